```python
import jax
import jax.numpy as jnp
from jax import lax
import numpy as np

D_MODEL = 2048
BATCH = 1
SEQ = 8192
DEPTH = 1
DEC_BATCH = 32
DEC_SEQ = 1
PAST_LEN = 16384
PAGE_SIZE = 128

N_META = 16
HEAD_DIM = 128
D_MIX = D_MODEL
H_A = D_MIX // (2 * HEAD_DIM)
DK_A = HEAD_DIM
DV_A = HEAD_DIM
H_B = D_MIX // (2 * HEAD_DIM)
D_B = HEAD_DIM
CONV_W = 4
CONV_DIM = H_A * (2 * DK_A + DV_A)
CHUNK_A = 64
SB_BLOCK = 128
D_FF = -(-8 * D_MODEL // (3 * 256)) * 256
IN_DIM = CONV_DIM + H_A * DV_A + 2 * H_A + 3 * H_B * D_B
EPS = 1e-6

kernel_name = 'hymba_gdn_stickbreaking_step'


def rmsnorm(x, g):
    xf = x.astype(jnp.float32)
    y = xf * lax.rsqrt(jnp.mean(xf * xf, axis=-1, keepdims=True) + EPS)
    return (y * g.astype(jnp.float32)).astype(x.dtype)


def l2norm(x):
    return x * lax.rsqrt(jnp.sum(x * x, axis=-1, keepdims=True) + EPS)


def split_proj(p):
    o1 = CONV_DIM
    o2 = o1 + H_A * DV_A
    o3 = o2 + H_A
    o4 = o3 + H_A
    o5 = o4 + H_B * D_B
    o6 = o5 + H_B * D_B
    return (p[..., :o1], p[..., o1:o2], p[..., o2:o3], p[..., o3:o4],
            p[..., o4:o5], p[..., o5:o6], p[..., o6:])


def causal_conv(buf, x, conv_w):
    t = x.shape[1]
    xp = jnp.concatenate([buf.astype(x.dtype), x], axis=1)
    y = xp[:, 0:t] * conv_w[0]
    for i in range(1, CONV_W):
        y = y + xp[:, i:i + t] * conv_w[i]
    return jax.nn.silu(y), xp[:, t:]


def gdn_chunked(q, k, v, g, beta, s0, chunk):
    bsz, t, h, _ = q.shape
    dv = v.shape[-1]
    n = t // chunk

    def blocks(x):
        return x.reshape((bsz, n, chunk, h) + x.shape[3:]).swapaxes(2, 3)

    q, k, v, g, beta = blocks(q), blocks(k), blocks(v), blocks(g), blocks(beta)
    gc = jnp.cumsum(g, axis=-1)
    incl = jnp.tril(jnp.ones((chunk, chunk), bool))
    strict = jnp.tril(jnp.ones((chunk, chunk), bool), -1)
    diff = gc[..., :, None] - gc[..., None, :]
    decay = jnp.where(incl, jnp.exp(jnp.where(incl, diff, 0.0)), 0.0)
    kb = k * beta[..., None]
    a = jnp.where(strict, jnp.einsum('bnhcd,bnhsd->bnhcs', kb, k) * decay, 0.0)
    m = a + jnp.eye(chunk, dtype=a.dtype)
    rhs = jnp.concatenate([v * beta[..., None], kb * jnp.exp(gc)[..., None]], axis=-1)
    sol = lax.linalg.triangular_solve(m, rhs, left_side=True, lower=True, unit_diagonal=True)
    u, w = sol[..., :dv], sol[..., dv:]
    qk = jnp.einsum('bnhcd,bnhsd->bnhcs', q, k) * decay
    q_dec = q * jnp.exp(gc)[..., None]
    k_dec = k * jnp.exp(gc[..., -1:] - gc)[..., None]
    g_tot = jnp.exp(gc[..., -1])

    def step(s, xs):
        u_c, w_c, qk_c, q_c, k_c, gt = xs
        v_new = u_c - jnp.einsum('bhcd,bhde->bhce', w_c, s)
        o = jnp.einsum('bhcd,bhde->bhce', q_c, s) + jnp.einsum('bhcs,bhse->bhce', qk_c, v_new)
        s = s * gt[..., None, None] + jnp.einsum('bhcd,bhce->bhde', k_c, v_new)
        return s, o

    xs = (jnp.moveaxis(u, 1, 0), jnp.moveaxis(w, 1, 0), jnp.moveaxis(qk, 1, 0),
          jnp.moveaxis(q_dec, 1, 0), jnp.moveaxis(k_dec, 1, 0), jnp.moveaxis(g_tot, 1, 0))
    s, o = lax.scan(step, s0, xs)
    o = jnp.moveaxis(o, 0, 1).swapaxes(2, 3).reshape(bsz, t, h, dv)
    return o, s


def gdn_mixer(qkv, z, b, a, conv_buf, s0, conv_w, a_log, dt_bias, gdn_norm, segments):
    bsz, t = qkv.shape[:2]
    y, new_buf = causal_conv(conv_buf, qkv, conv_w)
    y = y.astype(jnp.float32)
    q = l2norm(y[..., :H_A * DK_A].reshape(bsz, t, H_A, DK_A)) * DK_A ** -0.5
    k = l2norm(y[..., H_A * DK_A:2 * H_A * DK_A].reshape(bsz, t, H_A, DK_A))
    v = y[..., 2 * H_A * DK_A:].reshape(bsz, t, H_A, DV_A)
    beta = jax.nn.sigmoid(b.astype(jnp.float32))
    g = -jnp.exp(a_log.astype(jnp.float32)) * jax.nn.softplus(a.astype(jnp.float32) + dt_bias.astype(jnp.float32))
    s = s0.astype(jnp.float32)
    outs = []
    start = 0
    for length, chunk in segments:
        sl = slice(start, start + length)
        o, s = gdn_chunked(q[:, sl], k[:, sl], v[:, sl], g[:, sl], beta[:, sl], s, chunk)
        outs.append(o)
        start += length
    o = jnp.concatenate(outs, axis=1)
    zf = z.astype(jnp.float32).reshape(bsz, t, H_A, DV_A)
    o = rmsnorm(o, gdn_norm) * jax.nn.silu(zf)
    return o.reshape(bsz, t, H_A * DV_A).astype(qkv.dtype), s, new_buf


def sb_qkv(q_b, k_b, v_b, q_norm, k_norm):
    bsz, t = q_b.shape[:2]
    q = rmsnorm(q_b.reshape(bsz, t, H_B, D_B), q_norm)
    k = rmsnorm(k_b.reshape(bsz, t, H_B, D_B), k_norm)
    v = v_b.reshape(bsz, t, H_B, D_B)
    return q, k, v


def sb_scores(q, k, bias):
    z = jnp.einsum('bqhd,bkhd->bhqk', q, k, preferred_element_type=jnp.float32) * D_B ** -0.5
    return z + bias.astype(jnp.float32)[None, :, None, None]


def sb_weights(z, mask):
    log_nb = jnp.where(mask, jax.nn.log_sigmoid(-z), 0.0)
    suffix = lax.cumsum(log_nb, axis=z.ndim - 1, reverse=True) - log_nb
    return jnp.where(mask, jnp.exp(jax.nn.log_sigmoid(z) + suffix), 0.0)


def sb_read(a, v):
    return jnp.einsum('bhqk,bkhd->bqhd', a.astype(v.dtype), v, preferred_element_type=jnp.float32)


def sb_prompt(q, k, v, bias):
    bsz, t = q.shape[:2]
    kpos = jnp.arange(t)

    def attend(qb, qpos, kb, vb, kp):
        return sb_read(sb_weights(sb_scores(qb, kb, bias), kp[None, :] < qpos[:, None]), vb)

    meta_pos = jnp.arange(N_META)
    o_meta = attend(q[:, :N_META], meta_pos, k[:, :N_META], v[:, :N_META], meta_pos)
    nb = (t - N_META) // SB_BLOCK
    qr = jnp.moveaxis(q[:, N_META:].reshape(bsz, nb, SB_BLOCK, H_B, D_B), 1, 0)
    starts = N_META + SB_BLOCK * jnp.arange(nb)
    o_real = lax.map(lambda xs: attend(xs[0], xs[1] + jnp.arange(SB_BLOCK), k, v, kpos), (qr, starts))
    o_real = jnp.moveaxis(o_real, 0, 1).reshape(bsz, t - N_META, H_B, D_B)
    return jnp.concatenate([o_meta, o_real], axis=1)


def sb_sample(q, k, v, k_past, v_past, bias):
    past = k_past.shape[1]
    t = q.shape[1]
    z = jnp.concatenate([sb_scores(q, k_past, bias), sb_scores(q, k, bias)], axis=-1)
    qpos = past + jnp.arange(t)
    kpos = jnp.arange(past + t)
    a = sb_weights(z, kpos[None, :] < qpos[:, None])
    return sb_read(a[..., :past], v_past) + sb_read(a[..., past:], v)


def merge_and_ffn(x, o_a, o_b, sb_norm, w_out, norm_ffn, w_gate, w_up, w_down):
    bsz, t = x.shape[:2]
    o_b = rmsnorm(o_b, sb_norm).astype(x.dtype).reshape(bsz, t, H_B * D_B)
    x = x + jnp.concatenate([o_a, o_b], axis=-1) @ w_out
    h = rmsnorm(x, norm_ffn)
    return x + (jax.nn.silu(h @ w_gate) * (h @ w_up)) @ w_down


def setup_inputs(seed: int = 0) -> dict:
    key = jax.random.key(seed)
    ks = jax.random.split(key, 24)
    n_pages = PAST_LEN // PAGE_SIZE
    n_used = DEC_BATCH * n_pages
    n_pool = n_used + max(1, n_used // 4)

    def nrm(k, shape, s):
        return jax.random.normal(k, shape, jnp.float32) * s

    page_table = jax.random.permutation(ks[0], n_pool)[:n_used].reshape(DEC_BATCH, n_pages).astype(jnp.int32)
    dt = jnp.exp(jax.random.uniform(ks[1], (DEPTH, H_A), jnp.float32, np.log(1e-3), np.log(1e-1)))
    return {
        'x_prompt': nrm(ks[2], (BATCH, SEQ, D_MODEL), 1.0),
        'x_sample': nrm(ks[3], (DEC_BATCH, DEC_SEQ, D_MODEL), 1.0),
        'cache_k': nrm(ks[4], (DEPTH, n_pool, PAGE_SIZE, H_B, D_B), 1.0),
        'cache_v': nrm(ks[5], (DEPTH, n_pool, PAGE_SIZE, H_B, D_B), 1.0),
        'state_gdn': nrm(ks[6], (DEPTH, DEC_BATCH, H_A, DK_A, DV_A), 0.05),
        'state_conv': nrm(ks[7], (DEPTH, DEC_BATCH, CONV_W - 1, CONV_DIM), 1.0),
        'page_table': page_table,
        'meta': nrm(ks[8], (N_META, D_MODEL), 1.0),
        'norm_mix': 1.0 + nrm(ks[9], (DEPTH, D_MODEL), 0.02),
        'w_in': nrm(ks[10], (DEPTH, D_MODEL, IN_DIM), D_MODEL ** -0.5),
        'conv_w': nrm(ks[11], (DEPTH, CONV_W, CONV_DIM), CONV_W ** -0.5),
        'a_log': jnp.log(jax.random.uniform(ks[12], (DEPTH, H_A), jnp.float32, 1.0, 16.0)),
        'dt_bias': dt + jnp.log(-jnp.expm1(-dt)),
        'gdn_norm': 1.0 + nrm(ks[13], (DEPTH, DV_A), 0.02),
        'q_norm': 1.0 + nrm(ks[14], (DEPTH, D_B), 0.02),
        'k_norm': 1.0 + nrm(ks[15], (DEPTH, D_B), 0.02),
        'sb_norm': 1.0 + nrm(ks[16], (DEPTH, D_B), 0.02),
        'sb_bias': jax.random.uniform(ks[22], (DEPTH, H_B), jnp.float32, -8.0, -6.0),
        'w_out': nrm(ks[17], (DEPTH, D_MIX, D_MODEL), D_MIX ** -0.5),
        'norm_ffn': 1.0 + nrm(ks[18], (DEPTH, D_MODEL), 0.02),
        'w_gate': nrm(ks[19], (DEPTH, D_MODEL, D_FF), D_MODEL ** -0.5),
        'w_up': nrm(ks[20], (DEPTH, D_MODEL, D_FF), D_MODEL ** -0.5),
        'w_down': nrm(ks[21], (DEPTH, D_FF, D_MODEL), D_FF ** -0.5),
    }


def reference(x_prompt, x_sample, cache_k, cache_v, state_gdn, state_conv, page_table, meta,
              norm_mix, w_in, conv_w, a_log, dt_bias, gdn_norm, q_norm, k_norm, sb_norm, sb_bias,
              w_out, norm_ffn, w_gate, w_up, w_down):
    bsz = x_prompt.shape[0]
    dbsz = x_sample.shape[0]
    past = page_table.shape[1] * cache_k.shape[2]
    xp = jnp.concatenate([jnp.broadcast_to(meta.astype(x_prompt.dtype)[None], (bsz, N_META, D_MODEL)), x_prompt], axis=1)
    xs = x_sample
    tp = xp.shape[1]
    ts = xs.shape[1]
    pk, pv, pg, pc, sk, sv, sg, sc = [], [], [], [], [], [], [], []
    for l in range(DEPTH):
        qkv_a, z_a, b_a, a_a, q_b, k_b, v_b = split_proj(rmsnorm(xp, norm_mix[l]) @ w_in[l])
        conv0 = jnp.zeros((bsz, CONV_W - 1, CONV_DIM), xp.dtype)
        s0 = jnp.zeros((bsz, H_A, DK_A, DV_A), jnp.float32)
        o_a, s_fin, buf = gdn_mixer(qkv_a, z_a, b_a, a_a, conv0, s0, conv_w[l], a_log[l], dt_bias[l], gdn_norm[l],
                                    ((N_META, N_META), (tp - N_META, CHUNK_A)))
        q, k, v = sb_qkv(q_b, k_b, v_b, q_norm[l], k_norm[l])
        o_b = sb_prompt(q, k, v, sb_bias[l])
        xp = merge_and_ffn(xp, o_a, o_b, sb_norm[l], w_out[l], norm_ffn[l], w_gate[l], w_up[l], w_down[l])
        pk.append(k)
        pv.append(v)
        pg.append(s_fin)
        pc.append(buf)
        qkv_a, z_a, b_a, a_a, q_b, k_b, v_b = split_proj(rmsnorm(xs, norm_mix[l]) @ w_in[l])
        o_a, s_new, buf_new = gdn_mixer(qkv_a, z_a, b_a, a_a, state_conv[l], state_gdn[l], conv_w[l], a_log[l],
                                        dt_bias[l], gdn_norm[l], ((ts, ts),))
        q, k, v = sb_qkv(q_b, k_b, v_b, q_norm[l], k_norm[l])
        k_past = cache_k[l][page_table].reshape(dbsz, past, H_B, D_B)
        v_past = cache_v[l][page_table].reshape(dbsz, past, H_B, D_B)
        o_b = sb_sample(q, k, v, k_past, v_past, sb_bias[l])
        xs = merge_and_ffn(xs, o_a, o_b, sb_norm[l], w_out[l], norm_ffn[l], w_gate[l], w_up[l], w_down[l])
        sk.append(k)
        sv.append(v)
        sg.append(s_new)
        sc.append(buf_new)
    return (xp[:, N_META:], xs, jnp.stack(pk), jnp.stack(pv), jnp.stack(pg), jnp.stack(pc),
            jnp.stack(sk), jnp.stack(sv), jnp.stack(sg), jnp.stack(sc))
```

```python
import functools

import jax
import jax.numpy as jnp
from jax import lax
from jax.experimental import pallas as pl
from jax.experimental.pallas import tpu as pltpu

EPS = 1e-6
N_META = 16
HEAD_DIM = 128
CONV_W = 4
LOG2E = 1.4426950408889634
LN2 = 0.6931471805599453

SUBLANES = 8
LANES = 128
VMEM_LIMIT = 56 * 1024 * 1024

ROW_ALIGN = 768
TM_PROJ = 768
TM_OUT = 384
TF_FFN = 512
GDN_CHUNK = 128
TQ_ATT = 256
TK_ATT = 256
DEC_SLOTS = 8

f32 = jnp.float32
bf16 = jnp.bfloat16


def _dot(a, b):
    return jnp.dot(a, b, preferred_element_type=f32)


def _dot_nt(a, b):
    return lax.dot_general(a, b, (((1,), (1,)), ((), ())), preferred_element_type=f32)


def _dot_tn(a, b):
    return lax.dot_general(a, b, (((0,), (0,)), ((), ())), preferred_element_type=f32)


def _split3(x):
    hi = x.astype(bf16)
    r = x - hi.astype(f32)
    mid = r.astype(bf16)
    lo = (r - mid.astype(f32)).astype(bf16)
    return hi, mid, lo


def _dot3_left(x, m_bf16):
    hi, mid, lo = _split3(x)
    return _dot(hi, m_bf16) + _dot(mid, m_bf16) + _dot(lo, m_bf16)


def _dot3_right(m_bf16, x):
    hi, mid, lo = _split3(x)
    return _dot(m_bf16, hi) + _dot(m_bf16, mid) + _dot(m_bf16, lo)


def _sigmoid(x):
    return 1.0 / (1.0 + jnp.exp(-x))


def _softplus(x):
    return jnp.maximum(x, 0.0) + jnp.log1p(jnp.exp(-jnp.abs(x)))


def _softplus2(z2):
    return jnp.maximum(z2, 0.0) + jnp.log2(1.0 + jnp.exp2(-jnp.abs(z2)))


SOLVE_PASSES = 1
INV_BASE = 16


def _mm(a, b, passes):
    ah = a.astype(bf16)
    bh = b.astype(bf16)
    out = _dot(ah, bh)
    if passes == 3:
        al = (a - ah.astype(f32)).astype(bf16)
        bl = (b - bh.astype(f32)).astype(bf16)
        out = out + _dot(ah, bl) + _dot(al, bh)
    return out


def _unit_lower_inverse_minus_eye(a, ii, jj):
    C = a.shape[0]
    d = jnp.where((ii // INV_BASE) == (jj // INV_BASE), a, 0.0)
    n = -d
    p = d
    for _ in range(INV_BASE.bit_length() - 2):
        p = _mm(p, p, SOLVE_PASSES)
        n = n + p + _mm(n, p, SOLVE_PASSES)
    bs = INV_BASE
    while bs < C:
        off = jnp.where(jnp.logical_and((ii // (2 * bs)) == (jj // (2 * bs)), (ii // bs) != (jj // bs)), a, 0.0)
        x = off + _mm(off, n, SOLVE_PASSES)
        n = n - (x + _mm(n, x, SOLVE_PASSES))
        bs *= 2
    return n


def _iota2(shape, dim):
    return lax.broadcasted_iota(jnp.int32, shape, dim)


def _inproj_kernel(x_ref, g_ref, w_ref, wba_ref, wbat_ref, qn_ref, kn_ref,
                   p_ref, ba_ref, bat_ref, xn_scr, *, n_heads):
    j = pl.program_id(1)

    @pl.when(j == 0)
    def _():
        x = x_ref[...]
        xn = x * lax.rsqrt(jnp.mean(x * x, axis=-1, keepdims=True) + EPS) * g_ref[...]
        xnb = xn.astype(bf16)
        xn_scr[...] = xnb
        ba_ref[...] = _dot(xnb, wba_ref[...])
        bat_ref[...] = _dot_nt(wbat_ref[...], xnb)

    y = _dot(xn_scr[...], w_ref[...])

    def headnorm(gain_ref):
        for h in range(n_heads):
            sl = slice(h * HEAD_DIM, (h + 1) * HEAD_DIM)
            yh = y[:, sl]
            ms = jnp.mean(yh * yh, axis=-1, keepdims=True)
            p_ref[:, sl] = yh * lax.rsqrt(ms + EPS) * gain_ref[...]

    @pl.when(j == 4)
    def _():
        headnorm(qn_ref)

    @pl.when(j == 5)
    def _():
        headnorm(kn_ref)

    @pl.when(jnp.logical_and(j != 4, j != 5))
    def _():
        p_ref[...] = y


def _inproj(x_all, norm_mix, w_sec, w_ba, w_bat, q_norm, k_norm, n_heads):
    t_pad, d = x_all.shape
    hd = n_heads * HEAD_DIM
    tm = TM_PROJ
    grid = (t_pad // tm, 7)
    return pl.pallas_call(
        functools.partial(_inproj_kernel, n_heads=n_heads),
        grid=grid,
        in_specs=[
            pl.BlockSpec((tm, d), lambda i, j: (i, 0)),
            pl.BlockSpec((1, d), lambda i, j: (0, 0)),
            pl.BlockSpec((d, hd), lambda i, j: (0, j)),
            pl.BlockSpec((d, LANES), lambda i, j: (0, 0)),
            pl.BlockSpec((LANES, d), lambda i, j: (0, 0)),
            pl.BlockSpec((1, HEAD_DIM), lambda i, j: (0, 0)),
            pl.BlockSpec((1, HEAD_DIM), lambda i, j: (0, 0)),
        ],
        out_specs=[
            pl.BlockSpec((tm, hd), lambda i, j: (i, j)),
            pl.BlockSpec((tm, LANES), lambda i, j: (i, 0)),
            pl.BlockSpec((LANES, tm), lambda i, j: (0, i)),
        ],
        out_shape=[
            jax.ShapeDtypeStruct((t_pad, 7 * hd), f32),
            jax.ShapeDtypeStruct((t_pad, LANES), f32),
            jax.ShapeDtypeStruct((LANES, t_pad), f32),
        ],
        scratch_shapes=[pltpu.VMEM((tm, d), bf16)],
        compiler_params=pltpu.CompilerParams(
            dimension_semantics=("parallel", "arbitrary"), vmem_limit_bytes=VMEM_LIMIT),
    )(x_all, norm_mix, w_sec, w_ba, w_bat, q_norm, k_norm)


def _gdn_kernel(x_ref, ba_ref, bat_ref, cw_ref, alog_r_ref, dt_r_ref, alog_c_ref, dt_c_ref, gn_ref,
                oa_ref, s_ref, xbuf, *, n_heads, t_valid):
    c = pl.program_id(0)
    C = GDN_CHUNK
    hd = n_heads * HEAD_DIM
    hist = CONV_W - 1

    @pl.when(c == 0)
    def _():
        xbuf[0:SUBLANES, :] = jnp.zeros((SUBLANES, 3 * hd), f32)
        s_ref[...] = jnp.zeros(s_ref.shape, f32)

    xbuf[SUBLANES:SUBLANES + C, :] = x_ref[:, 0:3 * hd]
    y = xbuf[SUBLANES - hist:SUBLANES - hist + C, :] * cw_ref[0:1, :]
    for i in range(1, CONV_W):
        y = y + xbuf[SUBLANES - hist + i:SUBLANES - hist + i + C, :] * cw_ref[i:i + 1, :]
    y = y * _sigmoid(y)
    xbuf[SUBLANES - hist:SUBLANES, :] = xbuf[SUBLANES + C - hist:SUBLANES + C, :]

    ii = _iota2((C, C), 0)
    jj = _iota2((C, C), 1)
    incl = ii >= jj
    strict = ii > jj
    low_ones = jnp.where(incl, 1.0, 0.0).astype(bf16)
    up_ones = jnp.where(jj >= ii, 1.0, 0.0).astype(bf16)

    row_ok = (c * C + _iota2((C, LANES), 0)) < t_valid
    lane_ok = (c * C + _iota2((LANES, C), 1)) < t_valid
    ba = ba_ref[...]
    bat = bat_ref[...]
    beta_cols = jnp.where(row_ok, _sigmoid(ba), 0.0)
    g_cols = jnp.where(row_ok, -jnp.exp(alog_r_ref[...]) * _softplus(ba + dt_r_ref[...]), 0.0)
    g_rows = jnp.where(lane_ok, -jnp.exp(alog_c_ref[...]) * _softplus(bat + dt_c_ref[...]), 0.0)
    gc_cols = _dot3_right(low_ones, g_cols)
    gc_rows = _dot3_left(g_rows, up_ones)

    for h in range(n_heads):
        sl = slice(h * HEAD_DIM, (h + 1) * HEAD_DIM)
        q = y[:, h * HEAD_DIM:(h + 1) * HEAD_DIM]
        k = y[:, hd + h * HEAD_DIM:hd + (h + 1) * HEAD_DIM]
        v = y[:, 2 * hd + h * HEAD_DIM:2 * hd + (h + 1) * HEAD_DIM]
        z = x_ref[:, 3 * hd + h * HEAD_DIM:3 * hd + (h + 1) * HEAD_DIM]
        q = q * lax.rsqrt(jnp.sum(q * q, axis=-1, keepdims=True) + EPS) * (HEAD_DIM ** -0.5)
        k = k * lax.rsqrt(jnp.sum(k * k, axis=-1, keepdims=True) + EPS)

        beta = beta_cols[:, h:h + 1]
        gcc = gc_cols[:, n_heads + h:n_heads + h + 1]
        gcr = gc_rows[n_heads + h:n_heads + h + 1, :]
        g_last = gc_cols[C - 1:C, n_heads + h:n_heads + h + 1]
        decay = jnp.where(incl, jnp.exp(jnp.minimum(gcc - gcr, 0.0)), 0.0)
        kb = k * beta
        kbf = k.astype(bf16)
        a = jnp.where(strict, _dot_nt(kb.astype(bf16), kbf) * decay, 0.0)
        qk = _dot_nt(q.astype(bf16), kbf) * decay

        n = _unit_lower_inverse_minus_eye(a, ii, jj)
        egc = jnp.exp(gcc)
        rhs_u = v * beta
        rhs_w = kb * egc
        u = rhs_u + _mm(n, rhs_u, SOLVE_PASSES)
        w = rhs_w + _mm(n, rhs_w, SOLVE_PASSES)

        s = s_ref[h]
        sb = s.astype(bf16)
        v_new = u - _dot(w.astype(bf16), sb)
        vnb = v_new.astype(bf16)
        o = _dot((q * egc).astype(bf16), sb) + _dot(qk.astype(bf16), vnb)
        k_dec = k * jnp.exp(g_last - gcc)
        s_ref[h] = s * jnp.exp(g_last) + _dot_tn(k_dec.astype(bf16), vnb)

        o = o * lax.rsqrt(jnp.mean(o * o, axis=-1, keepdims=True) + EPS) * gn_ref[...]
        oa_ref[:, sl] = o * (z * _sigmoid(z))


def _gdn_prompt(p, ba, bat, conv_w, alog_r, dt_r, alog_c, dt_c, gdn_norm, n_heads, t_valid):
    t_pad = p.shape[0]
    hd = n_heads * HEAD_DIM
    C = GDN_CHUNK
    const = lambda c: (0, 0)
    return pl.pallas_call(
        functools.partial(_gdn_kernel, n_heads=n_heads, t_valid=t_valid),
        grid=(t_pad // C,),
        in_specs=[
            pl.BlockSpec((C, 4 * hd), lambda c: (c, 0)),
            pl.BlockSpec((C, LANES), lambda c: (c, 0)),
            pl.BlockSpec((LANES, C), lambda c: (0, c)),
            pl.BlockSpec((CONV_W, 3 * hd), const),
            pl.BlockSpec((1, LANES), const),
            pl.BlockSpec((1, LANES), const),
            pl.BlockSpec((LANES, LANES), const),
            pl.BlockSpec((LANES, LANES), const),
            pl.BlockSpec((1, HEAD_DIM), const),
        ],
        out_specs=[
            pl.BlockSpec((C, hd), lambda c: (c, 0)),
            pl.BlockSpec((n_heads, HEAD_DIM, HEAD_DIM), lambda c: (0, 0, 0)),
        ],
        out_shape=[
            jax.ShapeDtypeStruct((t_pad, hd), f32),
            jax.ShapeDtypeStruct((n_heads, HEAD_DIM, HEAD_DIM), f32),
        ],
        scratch_shapes=[pltpu.VMEM((SUBLANES + C, 3 * hd), f32)],
        compiler_params=pltpu.CompilerParams(
            dimension_semantics=("arbitrary",), vmem_limit_bytes=VMEM_LIMIT),
    )(p, ba, bat, conv_w, alog_r, dt_r, alog_c, dt_c, gdn_norm)


def _sb_attn_kernel(bias_ref, q_ref, k_ref, v_ref, gn_ref, o_ref, acc_scr, carry_scr, *, scale):
    h = pl.program_id(0)
    i = pl.program_id(1)
    tq, tk, G = TQ_ATT, TK_ATT, LANES
    bias2 = bias_ref[h] * LOG2E
    q2 = (q_ref[...] * (scale * LOG2E)).astype(bf16)

    r = _iota2((G, 2 * G), 0)
    cidx = _iota2((G, 2 * G), 1)
    mo = jnp.where(jnp.logical_or(cidx >= G, r >= cidx), 1.0, 0.0).astype(bf16)

    acc_scr[...] = jnp.zeros(acc_scr.shape, f32)
    carry_scr[...] = jnp.zeros(carry_scr.shape, f32)

    def span(start, masked):
        kblk = k_ref[pl.ds(start, tk), :].astype(bf16)
        vblk = v_ref[pl.ds(start, tk), :].astype(bf16)
        z2 = _dot_nt(q2, kblk) + bias2
        nl = _softplus2(z2)
        if masked:
            valid = _iota2((tq, tk), 1) < _iota2((tq, tk), 0)
            nl = jnp.where(valid, nl, 0.0)
        nlb = nl.astype(bf16)
        carry = carry_scr[...]
        parts = []
        for g in reversed(range(tk // G)):
            sg = _dot(nlb[:, g * G:(g + 1) * G], mo)
            expo = z2[:, g * G:(g + 1) * G] - sg[:, :G] - carry
            parts.append(jnp.exp2(expo))
            carry = carry + sg[:, G:]
        carry_scr[...] = carry
        a = jnp.concatenate(parts[::-1], axis=-1)
        if masked:
            a = jnp.where(valid, a, 0.0)
        acc_scr[...] += _dot(a.astype(bf16), vblk)

    span(pl.multiple_of(i * tq, tq), True)

    def body(t, _):
        span(pl.multiple_of((i - 1 - t) * tk, tk), False)
        return 0

    lax.fori_loop(0, i * (tq // tk), body, 0)

    o = acc_scr[...]
    o_ref[...] = o * lax.rsqrt(jnp.mean(o * o, axis=-1, keepdims=True) + EPS) * gn_ref[...]


def _sb_attn_prompt(p, sb_bias, sb_norm, n_heads):
    t_pad = p.shape[0]
    hd = n_heads * HEAD_DIM
    tq = TQ_ATT
    assert TQ_ATT == TK_ATT
    return pl.pallas_call(
        functools.partial(_sb_attn_kernel, scale=HEAD_DIM ** -0.5),
        grid=(n_heads, t_pad // tq),
        in_specs=[
            pl.BlockSpec(memory_space=pltpu.SMEM),
            pl.BlockSpec((tq, HEAD_DIM), lambda h, i: (i, 4 * n_heads + h)),
            pl.BlockSpec((t_pad, HEAD_DIM), lambda h, i: (0, 5 * n_heads + h)),
            pl.BlockSpec((t_pad, HEAD_DIM), lambda h, i: (0, 6 * n_heads + h)),
            pl.BlockSpec((1, HEAD_DIM), lambda h, i: (0, 0)),
        ],
        out_specs=pl.BlockSpec((tq, HEAD_DIM), lambda h, i: (i, h)),
        out_shape=jax.ShapeDtypeStruct((t_pad, hd), f32),
        scratch_shapes=[pltpu.VMEM((tq, HEAD_DIM), f32), pltpu.VMEM((tq, LANES), f32)],
        compiler_params=pltpu.CompilerParams(
            dimension_semantics=("parallel", "arbitrary"), vmem_limit_bytes=VMEM_LIMIT),
    )(sb_bias, p, p, p, sb_norm)


def _dec_attn_kernel(pt_ref, q_ref, bias_ref, msuf_ref, mtot_ref, ck_ref, cv_ref, o_ref,
                     buf, sem, z_scr, a_scr, *, n_heads, n_pages, n_batch, scale):
    b = pl.program_id(0)
    H = n_heads
    R = buf.shape[1]
    per_b = 2 * n_pages
    total = n_batch * per_b
    ns = DEC_SLOTS

    def copy(f, slot):
        bb = f // per_b
        jj = f % per_b
        page = pt_ref[bb, jj % n_pages]
        is_k = jj < n_pages
        return is_k, (pltpu.make_async_copy(ck_ref.at[page], buf.at[slot], sem.at[slot]),
                      pltpu.make_async_copy(cv_ref.at[page], buf.at[slot], sem.at[slot]))

    def start(f, slot):
        is_k, (ck, cv) = copy(f, slot)

        @pl.when(is_k)
        def _():
            ck.start()

        @pl.when(jnp.logical_not(is_k))
        def _():
            cv.start()

    def wait(slot):
        pltpu.make_async_copy(ck_ref.at[0], buf.at[slot], sem.at[slot]).wait()

    @pl.when(b == 0)
    def _():
        for s in range(ns):
            start(s, s)

    def advance(f, slot):
        @pl.when(f + ns < total)
        def _():
            start(f + ns, slot)

    qb = (q_ref[b] * (scale * LOG2E)).astype(bf16)
    own = (_iota2((H, R), 1) % H) == _iota2((H, R), 0)

    base = b * per_b

    def k_body(j, _):
        f = base + j
        slot = f % ns
        wait(slot)
        kp = buf[slot].astype(bf16)
        zz = _dot_nt(qb, kp)
        z_scr[pl.ds(j, 1), :] = jnp.sum(jnp.where(own, zz, 0.0), axis=0, keepdims=True)
        advance(f, slot)
        return 0

    lax.fori_loop(0, n_pages, k_body, 0)

    z2 = z_scr[...] + bias_ref[...] * LOG2E
    nl = _softplus2(z2)
    nlb = nl.astype(bf16)
    s_incl = _dot(nlb, msuf_ref[...])
    tot = _dot(nlb, mtot_ref[...])
    later = jnp.where(_iota2((n_pages, n_pages), 1) > _iota2((n_pages, n_pages), 0), 1.0, 0.0).astype(bf16)
    carry = _dot3_right(later, tot)
    a_scr[...] = jnp.exp2(z2 - s_incl - carry)

    def v_body(j, acc):
        f = base + n_pages + j
        slot = f % ns
        wait(slot)
        vp = buf[slot].astype(bf16)
        arow = jnp.broadcast_to(a_scr[pl.ds(j, 1), :], (H, R))
        acc = acc + _dot(jnp.where(own, arow, 0.0).astype(bf16), vp)
        advance(f, slot)
        return acc

    o_ref[b] = lax.fori_loop(0, n_pages, v_body, jnp.zeros((H, HEAD_DIM), f32))


def _dec_attn(page_table, q_s, bias_lanes, msuf, mtot, cache_k, cache_v, n_heads):
    n_batch, n_pages = page_table.shape
    R = cache_k.shape[1]
    assert n_heads == SUBLANES and (2 * n_pages) % DEC_SLOTS == 0
    const2 = lambda b, pt: (0, 0)
    grid_spec = pltpu.PrefetchScalarGridSpec(
        num_scalar_prefetch=1,
        grid=(n_batch,),
        in_specs=[
            pl.BlockSpec((n_batch, n_heads, HEAD_DIM), lambda b, pt: (0, 0, 0)),
            pl.BlockSpec((1, R), const2),
            pl.BlockSpec((R, R), const2),
            pl.BlockSpec((R, R), const2),
            pl.BlockSpec(memory_space=pl.ANY),
            pl.BlockSpec(memory_space=pl.ANY),
        ],
        out_specs=pl.BlockSpec((n_batch, n_heads, HEAD_DIM), lambda b, pt: (0, 0, 0)),
        scratch_shapes=[
            pltpu.VMEM((DEC_SLOTS, R, HEAD_DIM), f32),
            pltpu.SemaphoreType.DMA((DEC_SLOTS,)),
            pltpu.VMEM((n_pages, R), f32),
            pltpu.VMEM((n_pages, R), f32),
        ],
    )
    return pl.pallas_call(
        functools.partial(_dec_attn_kernel, n_heads=n_heads, n_pages=n_pages, n_batch=n_batch,
                          scale=HEAD_DIM ** -0.5),
        grid_spec=grid_spec,
        out_shape=jax.ShapeDtypeStruct((n_batch, n_heads, HEAD_DIM), f32),
        compiler_params=pltpu.CompilerParams(
            dimension_semantics=("arbitrary",), vmem_limit_bytes=VMEM_LIMIT),
    )(page_table, q_s, bias_lanes, msuf, mtot, cache_k, cache_v)


def _dec_finish_kernel(o_ref, q_ref, k_ref, v_ref, bias_ref, gn_ref, out_ref, *, past, dec_seq, scale):
    kpos = past + _iota2(o_ref.shape, 1) * 0 + (dec_seq - 1)
    qpos = past + _iota2(o_ref.shape, 1) * 0 + (dec_seq - 1)
    valid = kpos < qpos
    z = jnp.sum(q_ref[...] * k_ref[...], axis=-1, keepdims=True) * scale + bias_ref[...]
    beta = _sigmoid(z)
    o = o_ref[...]
    o = jnp.where(valid, o * (1.0 - beta) + beta * v_ref[...], o)
    out_ref[...] = o * lax.rsqrt(jnp.mean(o * o, axis=-1, keepdims=True) + EPS) * gn_ref[...]


def _dec_finish(o_past, q_s, k_s, v_s, bias_col, sb_norm, past, dec_seq):
    n = o_past.shape[0]
    full = pl.BlockSpec((n, HEAD_DIM), lambda i: (0, 0))
    return pl.pallas_call(
        functools.partial(_dec_finish_kernel, past=past, dec_seq=dec_seq, scale=HEAD_DIM ** -0.5),
        grid=(1,),
        in_specs=[full, full, full, full, full, pl.BlockSpec((1, HEAD_DIM), lambda i: (0, 0))],
        out_specs=full,
        out_shape=jax.ShapeDtypeStruct((n, HEAD_DIM), f32),
    )(o_past, q_s, k_s, v_s, bias_col, sb_norm)


def _gdn_step_kernel(x_ref, cs_ref, cw_ref, blog_ref, alog_ref, adec_ref, dtb_ref, gn_ref, s_ref,
                     o_ref, s_out_ref, *, n_heads):
    H = n_heads
    x = x_ref[0]
    cs = cs_ref[0]
    y = x[0:3 * H] * cw_ref[CONV_W - 1]
    for i in range(CONV_W - 1):
        y = y + cs[i] * cw_ref[i]
    y = y * _sigmoid(y)
    q = y[0:H]
    k = y[H:2 * H]
    v = y[2 * H:3 * H]
    z = x[3 * H:4 * H]
    q = q * lax.rsqrt(jnp.sum(q * q, axis=-1, keepdims=True) + EPS) * (HEAD_DIM ** -0.5)
    k = k * lax.rsqrt(jnp.sum(k * k, axis=-1, keepdims=True) + EPS)
    beta = _sigmoid(blog_ref[0])
    g = -jnp.exp(alog_ref[...]) * _softplus(adec_ref[0] + dtb_ref[...])
    eg = jnp.exp(g)

    eye = jnp.where(_iota2((HEAD_DIM, HEAD_DIM), 0) == _iota2((HEAD_DIM, HEAD_DIM), 1), 1.0, 0.0).astype(bf16)
    k3 = _split3(k)
    q3 = _split3(q)
    kcol = _dot_nt(eye, k3[0]) + _dot_nt(eye, k3[1]) + _dot_nt(eye, k3[2])
    qcol = _dot_nt(eye, q3[0]) + _dot_nt(eye, q3[1]) + _dot_nt(eye, q3[2])

    outs = []
    for h in range(H):
        s = s_ref[0, h]
        kc = kcol[:, h:h + 1]
        egh = eg[h:h + 1, :]
        ks = jnp.sum(kc * s, axis=0, keepdims=True)
        vn = (v[h:h + 1, :] - egh * ks) * beta[h:h + 1, :]
        s_new = s * egh + kc * vn
        s_out_ref[0, h] = s_new
        outs.append(jnp.sum(qcol[:, h:h + 1] * s_new, axis=0, keepdims=True))
    o = jnp.concatenate(outs, axis=0)
    o = o * lax.rsqrt(jnp.mean(o * o, axis=-1, keepdims=True) + EPS) * gn_ref[...]
    o_ref[0] = o * (z * _sigmoid(z))


def _gdn_step(x_s, conv_state, conv_w, blog, adec, alog, dtb, gdn_norm, state, n_heads):
    n_batch = x_s.shape[0]
    H = n_heads
    const2 = lambda b: (0, 0)
    return pl.pallas_call(
        functools.partial(_gdn_step_kernel, n_heads=n_heads),
        grid=(n_batch,),
        in_specs=[
            pl.BlockSpec((1, 4 * H, HEAD_DIM), lambda b: (b, 0, 0)),
            pl.BlockSpec((1, CONV_W - 1, 3 * H, HEAD_DIM), lambda b: (b, 0, 0, 0)),
            pl.BlockSpec((CONV_W, 3 * H, HEAD_DIM), lambda b: (0, 0, 0)),
            pl.BlockSpec((1, H, LANES), lambda b: (b, 0, 0)),
            pl.BlockSpec((H, LANES), const2),
            pl.BlockSpec((1, H, LANES), lambda b: (b, 0, 0)),
            pl.BlockSpec((H, LANES), const2),
            pl.BlockSpec((1, HEAD_DIM), const2),
            pl.BlockSpec((1, H, HEAD_DIM, HEAD_DIM), lambda b: (b, 0, 0, 0)),
        ],
        out_specs=[
            pl.BlockSpec((1, H, HEAD_DIM), lambda b: (b, 0, 0)),
            pl.BlockSpec((1, H, HEAD_DIM, HEAD_DIM), lambda b: (b, 0, 0, 0)),
        ],
        out_shape=[
            jax.ShapeDtypeStruct((n_batch, H, HEAD_DIM), f32),
            jax.ShapeDtypeStruct((n_batch, H, HEAD_DIM, HEAD_DIM), f32),
        ],
        compiler_params=pltpu.CompilerParams(dimension_semantics=("parallel",)),
    )(x_s, conv_state, conv_w, blog, alog, adec, dtb, gdn_norm, state)


def _outproj_kernel(x_ref, oa_ref, ob_ref, wa_ref, wb_ref, g_ref, x1_ref, hn_ref):
    x1 = x_ref[...] + _dot(oa_ref[...].astype(bf16), wa_ref[...]) + _dot(ob_ref[...].astype(bf16), wb_ref[...])
    x1_ref[...] = x1
    hn = x1 * lax.rsqrt(jnp.mean(x1 * x1, axis=-1, keepdims=True) + EPS) * g_ref[...]
    hn_ref[...] = hn.astype(bf16)


def _outproj(x_all, oa, ob, w_a, w_b, norm_ffn):
    t_pad, d = x_all.shape
    hd = oa.shape[1]
    tm = TM_OUT
    return pl.pallas_call(
        _outproj_kernel,
        grid=(t_pad // tm,),
        in_specs=[
            pl.BlockSpec((tm, d), lambda i: (i, 0)),
            pl.BlockSpec((tm, hd), lambda i: (i, 0)),
            pl.BlockSpec((tm, hd), lambda i: (i, 0)),
            pl.BlockSpec((hd, d), lambda i: (0, 0)),
            pl.BlockSpec((hd, d), lambda i: (0, 0)),
            pl.BlockSpec((1, d), lambda i: (0, 0)),
        ],
        out_specs=[pl.BlockSpec((tm, d), lambda i: (i, 0)), pl.BlockSpec((tm, d), lambda i: (i, 0))],
        out_shape=[jax.ShapeDtypeStruct((t_pad, d), f32), jax.ShapeDtypeStruct((t_pad, d), bf16)],
        compiler_params=pltpu.CompilerParams(
            dimension_semantics=("parallel",), vmem_limit_bytes=VMEM_LIMIT),
    )(x_all, oa, ob, w_a, w_b, norm_ffn)


def _ffn_kernel(x1_ref, hn_ref, wg_ref, wu_ref, wd_ref, y_ref):
    f = pl.program_id(1)

    @pl.when(f == 0)
    def _():
        y_ref[...] = x1_ref[...]

    hn = hn_ref[...]
    gate = _dot(hn, wg_ref[...])
    up = _dot(hn, wu_ref[...])
    act = (gate * _sigmoid(gate) * up).astype(bf16)
    y_ref[...] += _dot(act, wd_ref[...])


def _ffn(x1, hn, w_gate, w_up, w_down):
    t_pad, d = x1.shape
    d_ff = w_gate.shape[1]
    tm, tf = TM_PROJ, TF_FFN
    assert d_ff % tf == 0
    return pl.pallas_call(
        _ffn_kernel,
        grid=(t_pad // tm, d_ff // tf),
        in_specs=[
            pl.BlockSpec((tm, d), lambda i, f: (i, 0)),
            pl.BlockSpec((tm, d), lambda i, f: (i, 0)),
            pl.BlockSpec((d, tf), lambda i, f: (0, f)),
            pl.BlockSpec((d, tf), lambda i, f: (0, f)),
            pl.BlockSpec((tf, d), lambda i, f: (f, 0)),
        ],
        out_specs=pl.BlockSpec((tm, d), lambda i, f: (i, 0)),
        out_shape=jax.ShapeDtypeStruct((t_pad, d), f32),
        compiler_params=pltpu.CompilerParams(
            dimension_semantics=("parallel", "arbitrary"), vmem_limit_bytes=VMEM_LIMIT),
    )(x1, hn, w_gate, w_up, w_down)


def _lane_pad(v, offset):
    return jnp.zeros((1, LANES), f32).at[0, offset:offset + v.shape[0]].set(v)


def kernel(x_prompt, x_sample, cache_k, cache_v, state_gdn, state_conv, page_table, meta, norm_mix, w_in,
           conv_w, a_log, dt_bias, gdn_norm, q_norm, k_norm, sb_norm, sb_bias, w_out, norm_ffn, w_gate,
           w_up, w_down):
    depth = w_in.shape[0]
    bsz, seq, d = x_prompt.shape
    dbsz, dec_seq, _ = x_sample.shape
    assert depth == 1 and bsz == 1 and dec_seq == 1
    H = d // (2 * HEAD_DIM)
    hd = H * HEAD_DIM
    n_pages, page_size = page_table.shape[1], cache_k.shape[2]
    past = n_pages * page_size
    t_valid = N_META + seq
    t_used = t_valid + dbsz
    t_pad = -(-t_used // ROW_ALIGN) * ROW_ALIGN
    l = 0

    x_all = jnp.concatenate(
        [meta.astype(f32), x_prompt[0], x_sample[:, 0], jnp.zeros((t_pad - t_used, d), f32)], axis=0)

    w = w_in[l]
    o_z = 3 * hd
    o_b = o_z + hd
    o_q = o_b + 2 * H
    w_sec = jnp.concatenate([w[:, :o_b], w[:, o_q:]], axis=1).astype(bf16)
    w_ba = jnp.concatenate([w[:, o_b:o_q], jnp.zeros((d, LANES - 2 * H), f32)], axis=1).astype(bf16)

    p, ba, bat = _inproj(x_all, norm_mix[l][None], w_sec, w_ba, w_ba.T, q_norm[l][None], k_norm[l][None], H)

    cw = conv_w[l]
    alog_r = _lane_pad(a_log[l], H)
    dt_r = _lane_pad(dt_bias[l], H)
    oa, s_fin = _gdn_prompt(p, ba, bat, cw, alog_r, dt_r,
                            jnp.broadcast_to(alog_r.T, (LANES, LANES)), jnp.broadcast_to(dt_r.T, (LANES, LANES)),
                            gdn_norm[l][None], H, t_valid)

    ob = _sb_attn_prompt(p, sb_bias[l], sb_norm[l][None], H)

    ps_rows = p[t_valid:t_used]
    ba_s = ba[t_valid:t_used]
    x_s = ps_rows[:, :4 * hd].reshape(dbsz, 4 * H, HEAD_DIM)
    blog = jnp.broadcast_to(ba_s[:, :H, None], (dbsz, H, LANES))
    adec = jnp.broadcast_to(ba_s[:, H:2 * H, None], (dbsz, H, LANES))
    oa_s, s_new = _gdn_step(
        x_s, state_conv[l].reshape(dbsz, CONV_W - 1, 3 * H, HEAD_DIM), cw.reshape(CONV_W, 3 * H, HEAD_DIM),
        blog, adec, jnp.broadcast_to(a_log[l][:, None], (H, LANES)),
        jnp.broadcast_to(dt_bias[l][:, None], (H, LANES)), gdn_norm[l][None], state_gdn[l], H)

    q_s = ps_rows[:, 4 * hd:5 * hd]
    k_s = ps_rows[:, 5 * hd:6 * hd]
    v_s = ps_rows[:, 6 * hd:7 * hd]
    R = page_size * H
    ridx = jnp.arange(R)
    same_head = (ridx[:, None] % H) == (ridx[None, :] % H)
    msuf = jnp.logical_and(same_head, (ridx[:, None] // H) >= (ridx[None, :] // H)).astype(bf16)
    mtot = same_head.astype(bf16)
    bias_lanes = jnp.tile(sb_bias[l], page_size)[None]
    o_past = _dec_attn(page_table, q_s.reshape(dbsz, H, HEAD_DIM), bias_lanes, msuf, mtot,
                       cache_k[l].reshape(-1, R, HEAD_DIM), cache_v[l].reshape(-1, R, HEAD_DIM), H)
    bias_col = jnp.broadcast_to(jnp.tile(sb_bias[l], dbsz)[:, None], (dbsz * H, HEAD_DIM))
    ob_s = _dec_finish(o_past.reshape(dbsz * H, HEAD_DIM), q_s.reshape(dbsz * H, HEAD_DIM),
                       k_s.reshape(dbsz * H, HEAD_DIM), v_s.reshape(dbsz * H, HEAD_DIM),
                       bias_col, sb_norm[l][None], past, dec_seq)

    oa = lax.dynamic_update_slice(oa, oa_s.reshape(dbsz, hd), (t_valid, 0))
    ob = lax.dynamic_update_slice(ob, ob_s.reshape(dbsz, hd), (t_valid, 0))

    wo = w_out[l].astype(bf16)
    x1, hn = _outproj(x_all, oa, ob, wo[:hd], wo[hd:], norm_ffn[l][None])
    y = _ffn(x1, hn, w_gate[l].astype(bf16), w_up[l].astype(bf16), w_down[l].astype(bf16))

    new_conv_s = jnp.concatenate([state_conv[l][:, 1:], ps_rows[:, None, :3 * hd]], axis=1)
    return (
        y[N_META:t_valid][None],
        y[t_valid:t_used][:, None],
        p[:t_valid, 5 * hd:6 * hd].reshape(1, 1, t_valid, H, HEAD_DIM),
        p[:t_valid, 6 * hd:7 * hd].reshape(1, 1, t_valid, H, HEAD_DIM),
        s_fin[None, None],
        p[t_valid - (CONV_W - 1):t_valid, :3 * hd][None, None],
        k_s.reshape(1, dbsz, 1, H, HEAD_DIM),
        v_s.reshape(1, dbsz, 1, H, HEAD_DIM),
        s_new[None],
        new_conv_s[None],
    )
```

```python
import functools

import jax
import jax.numpy as jnp
from jax import lax
from jax.experimental import pallas as pl
from jax.experimental.pallas import tpu as pltpu

EPS = 1e-6
N_META = 16
HEAD_DIM = 128
CONV_W = 4
LOG2E = 1.4426950408889634
LN2 = 0.6931471805599453

SUBLANES = 8
LANES = 128
VMEM_LIMIT = 56 * 1024 * 1024

ROW_ALIGN = 768
TM_PROJ = 768
TM_OUT = 384
TF_FFN = 512
GDN_CHUNK = 128
TQ_ATT = 768
TK_ATT = 768
ATT_GROUP = 256
DEC_SLOTS = 16
DEC_UNROLL = 8

f32 = jnp.float32
bf16 = jnp.bfloat16


def _dot(a, b):
    return jnp.dot(a, b, preferred_element_type=f32)


def _dot_nt(a, b):
    return lax.dot_general(a, b, (((1,), (1,)), ((), ())), preferred_element_type=f32)


def _dot_tn(a, b):
    return lax.dot_general(a, b, (((0,), (0,)), ((), ())), preferred_element_type=f32)


def _split3(x):
    hi = x.astype(bf16)
    r = x - hi.astype(f32)
    mid = r.astype(bf16)
    lo = (r - mid.astype(f32)).astype(bf16)
    return hi, mid, lo


def _dot3_left(x, m_bf16):
    hi, mid, lo = _split3(x)
    return _dot(hi, m_bf16) + _dot(mid, m_bf16) + _dot(lo, m_bf16)


def _dot3_right(m_bf16, x):
    hi, mid, lo = _split3(x)
    return _dot(m_bf16, hi) + _dot(m_bf16, mid) + _dot(m_bf16, lo)


def _sigmoid(x):
    return 1.0 / (1.0 + jnp.exp(-x))


def _softplus(x):
    return jnp.maximum(x, 0.0) + jnp.log(1.0 + jnp.exp(-jnp.abs(x)))


def _softplus2(z2):
    return jnp.maximum(z2, 0.0) + jnp.log2(1.0 + jnp.exp2(-jnp.abs(z2)))


INV_BASE = 16


def _unit_lower_inverse_minus_eye(mats, ii, jj):
    C = mats[0].shape[0]
    idx = range(len(mats))
    diag = (ii // INV_BASE) == (jj // INV_BASE)
    d = [jnp.where(diag, a, 0.0) for a in mats]
    n = [-x for x in d]
    p = d
    for _ in range(INV_BASE.bit_length() - 2):
        pb = [x.astype(bf16) for x in p]
        p = [_dot(x, x) for x in pb]
        pb = [x.astype(bf16) for x in p]
        n = [n[i] + p[i] + _dot(n[i].astype(bf16), pb[i]) for i in idx]
    bs = INV_BASE
    while bs < C:
        pair = jnp.logical_and((ii // (2 * bs)) == (jj // (2 * bs)), (ii // bs) != (jj // bs))
        off = [jnp.where(pair, a, 0.0) for a in mats]
        x = [off[i] + _dot(off[i].astype(bf16), n[i].astype(bf16)) for i in idx]
        n = [n[i] - (x[i] + _dot(n[i].astype(bf16), x[i].astype(bf16))) for i in idx]
        bs *= 2
    return n


def _iota2(shape, dim):
    return lax.broadcasted_iota(jnp.int32, shape, dim)


def _inproj_kernel(x_ref, g_ref, w_ref, wba_ref, wbat_ref, qn_ref, kn_ref,
                   p_ref, ba_ref, bat_ref, xn_scr, *, n_heads):
    j = pl.program_id(1)

    @pl.when(j == 0)
    def _():
        x = x_ref[...]
        xn = x * lax.rsqrt(jnp.mean(x * x, axis=-1, keepdims=True) + EPS) * g_ref[...]
        xnb = xn.astype(bf16)
        xn_scr[...] = xnb
        ba_ref[...] = _dot(xnb, wba_ref[...])
        bat_ref[...] = _dot_nt(wbat_ref[...], xnb)

    y = _dot(xn_scr[...], w_ref[...])

    def headnorm(gain_ref):
        for h in range(n_heads):
            sl = slice(h * HEAD_DIM, (h + 1) * HEAD_DIM)
            yh = y[:, sl]
            ms = jnp.mean(yh * yh, axis=-1, keepdims=True)
            p_ref[:, sl] = yh * lax.rsqrt(ms + EPS) * gain_ref[...]

    @pl.when(j == 4)
    def _():
        headnorm(qn_ref)

    @pl.when(j == 5)
    def _():
        headnorm(kn_ref)

    @pl.when(jnp.logical_and(j != 4, j != 5))
    def _():
        p_ref[...] = y


def _inproj(x_all, norm_mix, w_sec, w_ba, w_bat, q_norm, k_norm, n_heads):
    t_pad, d = x_all.shape
    hd = n_heads * HEAD_DIM
    tm = TM_PROJ
    grid = (t_pad // tm, 7)
    return pl.pallas_call(
        functools.partial(_inproj_kernel, n_heads=n_heads),
        grid=grid,
        in_specs=[
            pl.BlockSpec((tm, d), lambda i, j: (i, 0)),
            pl.BlockSpec((1, d), lambda i, j: (0, 0)),
            pl.BlockSpec((d, hd), lambda i, j: (0, j)),
            pl.BlockSpec((d, LANES), lambda i, j: (0, 0)),
            pl.BlockSpec((LANES, d), lambda i, j: (0, 0)),
            pl.BlockSpec((1, HEAD_DIM), lambda i, j: (0, 0)),
            pl.BlockSpec((1, HEAD_DIM), lambda i, j: (0, 0)),
        ],
        out_specs=[
            pl.BlockSpec((tm, hd), lambda i, j: (i, j)),
            pl.BlockSpec((tm, LANES), lambda i, j: (i, 0)),
            pl.BlockSpec((LANES, tm), lambda i, j: (0, i)),
        ],
        out_shape=[
            jax.ShapeDtypeStruct((t_pad, 7 * hd), f32),
            jax.ShapeDtypeStruct((t_pad, LANES), f32),
            jax.ShapeDtypeStruct((LANES, t_pad), f32),
        ],
        scratch_shapes=[pltpu.VMEM((tm, d), bf16)],
        compiler_params=pltpu.CompilerParams(
            dimension_semantics=("parallel", "arbitrary"), vmem_limit_bytes=VMEM_LIMIT),
    )(x_all, norm_mix, w_sec, w_ba, w_bat, q_norm, k_norm)


def _gdn_kernel(x_ref, ba_ref, bat_ref, cw_ref, alog_r_ref, dt_r_ref, alog_c_ref, dt_c_ref, gn_ref,
                oa_ref, s_ref, xbuf, *, n_heads, t_valid):
    c = pl.program_id(0)
    C = GDN_CHUNK
    hd = n_heads * HEAD_DIM
    hist = CONV_W - 1

    @pl.when(c == 0)
    def _():
        xbuf[0:SUBLANES, :] = jnp.zeros((SUBLANES, 3 * hd), f32)
        s_ref[...] = jnp.zeros(s_ref.shape, f32)

    xbuf[SUBLANES:SUBLANES + C, :] = x_ref[:, 0:3 * hd]
    y = xbuf[SUBLANES - hist:SUBLANES - hist + C, :] * cw_ref[0:1, :]
    for i in range(1, CONV_W):
        y = y + xbuf[SUBLANES - hist + i:SUBLANES - hist + i + C, :] * cw_ref[i:i + 1, :]
    y = y * _sigmoid(y)
    xbuf[SUBLANES - hist:SUBLANES, :] = xbuf[SUBLANES + C - hist:SUBLANES + C, :]

    ii = _iota2((C, C), 0)
    jj = _iota2((C, C), 1)
    incl = ii >= jj
    strict = ii > jj
    low_ones = jnp.where(incl, 1.0, 0.0).astype(bf16)
    up_ones = jnp.where(jj >= ii, 1.0, 0.0).astype(bf16)

    row_ok = (c * C + _iota2((C, LANES), 0)) < t_valid
    nr = 2 * n_heads
    lane_ok = (c * C + _iota2((nr, C), 1)) < t_valid
    ba = ba_ref[...]
    bat = bat_ref[0:nr, :]
    beta_cols = jnp.where(row_ok, _sigmoid(ba), 0.0)
    g_cols = jnp.where(row_ok, -jnp.exp(alog_r_ref[...]) * _softplus(ba + dt_r_ref[...]), 0.0)
    g_rows = jnp.where(lane_ok, -jnp.exp(alog_c_ref[0:nr, :]) * _softplus(bat + dt_c_ref[0:nr, :]), 0.0)
    gc_cols = _dot3_right(low_ones, g_cols)
    gc_rows = _dot3_left(g_rows, up_ones)

    heads = range(n_heads)

    def head_cols(off, h):
        return slice(off + h * HEAD_DIM, off + (h + 1) * HEAD_DIM)

    q = [y[:, head_cols(0, h)] for h in heads]
    k = [y[:, head_cols(hd, h)] for h in heads]
    v = [y[:, head_cols(2 * hd, h)] for h in heads]
    q = [x * lax.rsqrt(jnp.sum(x * x, axis=-1, keepdims=True) + EPS) * (HEAD_DIM ** -0.5) for x in q]
    k = [x * lax.rsqrt(jnp.sum(x * x, axis=-1, keepdims=True) + EPS) for x in k]

    beta = [beta_cols[:, h:h + 1] for h in heads]
    gcc = [gc_cols[:, n_heads + h:n_heads + h + 1] for h in heads]
    gcr = [gc_rows[n_heads + h:n_heads + h + 1, :] for h in heads]
    g_last = [gc_cols[C - 1:C, n_heads + h:n_heads + h + 1] for h in heads]
    decay = [jnp.where(incl, jnp.exp(jnp.minimum(gcc[h] - gcr[h], 0.0)), 0.0) for h in heads]
    kb = [k[h] * beta[h] for h in heads]
    kbf = [k[h].astype(bf16) for h in heads]
    a = [jnp.where(strict, _dot_nt(kb[h].astype(bf16), kbf[h]) * decay[h], 0.0) for h in heads]
    qk = [(_dot_nt(q[h].astype(bf16), kbf[h]) * decay[h]).astype(bf16) for h in heads]

    n = _unit_lower_inverse_minus_eye(a, ii, jj)
    egc = [jnp.exp(gcc[h]) for h in heads]
    rhs_u = [v[h] * beta[h] for h in heads]
    rhs_w = [kb[h] * egc[h] for h in heads]
    nb = [n[h].astype(bf16) for h in heads]
    u = [rhs_u[h] + _dot(nb[h], rhs_u[h].astype(bf16)) for h in heads]
    w = [(rhs_w[h] + _dot(nb[h], rhs_w[h].astype(bf16))).astype(bf16) for h in heads]
    q_dec = [(q[h] * egc[h]).astype(bf16) for h in heads]
    k_dec = [(k[h] * jnp.exp(g_last[h] - gcc[h])).astype(bf16) for h in heads]

    s = [s_ref[h] for h in heads]
    sb = [s[h].astype(bf16) for h in heads]
    vnb = [(u[h] - _dot(w[h], sb[h])).astype(bf16) for h in heads]
    o = [_dot(q_dec[h], sb[h]) + _dot(qk[h], vnb[h]) for h in heads]
    for h in heads:
        s_ref[h] = s[h] * jnp.exp(g_last[h]) + _dot_tn(k_dec[h], vnb[h])
    for h in heads:
        z = x_ref[:, head_cols(3 * hd, h)]
        on = o[h] * lax.rsqrt(jnp.mean(o[h] * o[h], axis=-1, keepdims=True) + EPS) * gn_ref[...]
        oa_ref[:, head_cols(0, h)] = on * (z * _sigmoid(z))


def _gdn_prompt(p, ba, bat, conv_w, alog_r, dt_r, alog_c, dt_c, gdn_norm, n_heads, t_valid):
    t_pad = p.shape[0]
    hd = n_heads * HEAD_DIM
    C = GDN_CHUNK
    const = lambda c: (0, 0)
    return pl.pallas_call(
        functools.partial(_gdn_kernel, n_heads=n_heads, t_valid=t_valid),
        grid=(t_pad // C,),
        in_specs=[
            pl.BlockSpec((C, 4 * hd), lambda c: (c, 0)),
            pl.BlockSpec((C, LANES), lambda c: (c, 0)),
            pl.BlockSpec((LANES, C), lambda c: (0, c)),
            pl.BlockSpec((CONV_W, 3 * hd), const),
            pl.BlockSpec((1, LANES), const),
            pl.BlockSpec((1, LANES), const),
            pl.BlockSpec((LANES, LANES), const),
            pl.BlockSpec((LANES, LANES), const),
            pl.BlockSpec((1, HEAD_DIM), const),
        ],
        out_specs=[
            pl.BlockSpec((C, hd), lambda c: (c, 0)),
            pl.BlockSpec((n_heads, HEAD_DIM, HEAD_DIM), lambda c: (0, 0, 0)),
        ],
        out_shape=[
            jax.ShapeDtypeStruct((t_pad, hd), f32),
            jax.ShapeDtypeStruct((n_heads, HEAD_DIM, HEAD_DIM), f32),
        ],
        scratch_shapes=[pltpu.VMEM((SUBLANES + C, 3 * hd), f32)],
        compiler_params=pltpu.CompilerParams(
            dimension_semantics=("arbitrary",), vmem_limit_bytes=VMEM_LIMIT),
    )(p, ba, bat, conv_w, alog_r, dt_r, alog_c, dt_c, gdn_norm)


def _sb_attn_kernel(bias_ref, q_ref, k_ref, v_ref, gn_ref, o_ref, acc_scr, carry_scr, *, scale):
    h = pl.program_id(0)
    i = pl.program_id(1)
    tq, tk, G = TQ_ATT, TK_ATT, ATT_GROUP
    bias2 = bias_ref[h] * LOG2E
    q2 = (q_ref[...] * (scale * LOG2E)).astype(bf16)

    mo = jnp.where(_iota2((G, G), 0) >= _iota2((G, G), 1), 1.0, 0.0).astype(bf16)

    acc_scr[...] = jnp.zeros(acc_scr.shape, f32)
    carry_scr[...] = jnp.zeros(carry_scr.shape, f32)

    def span(start, masked):
        kblk = k_ref[pl.ds(start, tk), :].astype(bf16)
        vblk = v_ref[pl.ds(start, tk), :].astype(bf16)
        z2 = _dot_nt(q2, kblk) + bias2
        nl = _softplus2(z2)
        if masked:
            valid = _iota2((tq, tk), 1) < _iota2((tq, tk), 0)
            nl = jnp.where(valid, nl, 0.0)
        nlb = nl.astype(bf16)
        groups = range(tk // G)
        sg = [_dot(nlb[:, g * G:(g + 1) * G], mo) for g in groups]
        carries = [None] * len(groups)
        carry = carry_scr[:, 0:1]
        for g in reversed(groups):
            carries[g] = carry
            carry = carry + sg[g][:, 0:1]
        carry_scr[...] = jnp.broadcast_to(carry, carry_scr.shape)
        a = jnp.concatenate(
            [jnp.exp2(z2[:, g * G:(g + 1) * G] - sg[g] - carries[g]) for g in groups], axis=-1)
        if masked:
            a = jnp.where(valid, a, 0.0)
        acc_scr[...] += _dot(a.astype(bf16), vblk)

    span(pl.multiple_of(i * tq, tq), True)

    def body(t, _):
        span(pl.multiple_of((i - 1 - t) * tk, tk), False)
        return 0

    lax.fori_loop(0, i * (tq // tk), body, 0)

    o = acc_scr[...]
    o_ref[...] = o * lax.rsqrt(jnp.mean(o * o, axis=-1, keepdims=True) + EPS) * gn_ref[...]


def _sb_attn_prompt(p, sb_bias, sb_norm, n_heads):
    t_pad = p.shape[0]
    hd = n_heads * HEAD_DIM
    tq = TQ_ATT
    assert TQ_ATT == TK_ATT
    return pl.pallas_call(
        functools.partial(_sb_attn_kernel, scale=HEAD_DIM ** -0.5),
        grid=(n_heads, t_pad // tq),
        in_specs=[
            pl.BlockSpec(memory_space=pltpu.SMEM),
            pl.BlockSpec((tq, HEAD_DIM), lambda h, i: (i, 4 * n_heads + h)),
            pl.BlockSpec((t_pad, HEAD_DIM), lambda h, i: (0, 5 * n_heads + h)),
            pl.BlockSpec((t_pad, HEAD_DIM), lambda h, i: (0, 6 * n_heads + h)),
            pl.BlockSpec((1, HEAD_DIM), lambda h, i: (0, 0)),
        ],
        out_specs=pl.BlockSpec((tq, HEAD_DIM), lambda h, i: (i, h)),
        out_shape=jax.ShapeDtypeStruct((t_pad, hd), f32),
        scratch_shapes=[pltpu.VMEM((tq, HEAD_DIM), f32), pltpu.VMEM((tq, LANES), f32)],
        compiler_params=pltpu.CompilerParams(
            dimension_semantics=("parallel", "arbitrary"), vmem_limit_bytes=VMEM_LIMIT),
    )(sb_bias, p, p, p, sb_norm)


def _dec_attn_kernel(pt_ref, q_ref, bias_ref, msuf_ref, mtot_ref, ck_ref, cv_ref, o_ref,
                     buf, sem, z_scr, a_scr, *, n_heads, n_pages, n_batch, scale):
    b = pl.program_id(0)
    H = n_heads
    R = buf.shape[1]
    per_b = 2 * n_pages
    total = n_batch * per_b
    ns = DEC_SLOTS

    def copy(f, slot):
        bb = f // per_b
        jj = f % per_b
        page = pt_ref[bb, jj % n_pages]
        is_k = jj < n_pages
        return is_k, (pltpu.make_async_copy(ck_ref.at[page], buf.at[slot], sem.at[slot]),
                      pltpu.make_async_copy(cv_ref.at[page], buf.at[slot], sem.at[slot]))

    def start(f, slot):
        is_k, (ck, cv) = copy(f, slot)

        @pl.when(is_k)
        def _():
            ck.start()

        @pl.when(jnp.logical_not(is_k))
        def _():
            cv.start()

    def wait(slot):
        pltpu.make_async_copy(ck_ref.at[0], buf.at[slot], sem.at[slot]).wait()

    @pl.when(b == 0)
    def _():
        for s in range(ns):
            start(s, s)

    def advance(f, slot):
        @pl.when(f + ns < total)
        def _():
            start(f + ns, slot)

    qb = (q_ref[b] * (scale * LOG2E)).astype(bf16)
    own = (_iota2((H, R), 1) % H) == _iota2((H, R), 0)

    base = b * per_b

    U = DEC_UNROLL
    group = range(U)

    def k_body(jo, _):
        j0 = pl.multiple_of(jo * U, U)
        slots = [(base + j0 + u) % ns for u in group]
        for u in group:
            wait(slots[u])
        kp = [buf[slots[u]].astype(bf16) for u in group]
        zz = [_dot_nt(qb, kp[u]) for u in group]
        rows = [jnp.sum(jnp.where(own, zz[u], 0.0), axis=0, keepdims=True) for u in group]
        z_scr[pl.ds(j0, U), :] = jnp.concatenate(rows, axis=0)
        for u in group:
            advance(base + j0 + u, slots[u])
        return 0

    lax.fori_loop(0, n_pages // U, k_body, 0)

    z2 = z_scr[...] + bias_ref[...] * LOG2E
    nl = _softplus2(z2)
    nlb = nl.astype(bf16)
    s_incl = _dot(nlb, msuf_ref[...])
    tot = _dot(nlb, mtot_ref[...])
    later = jnp.where(_iota2((n_pages, n_pages), 1) > _iota2((n_pages, n_pages), 0), 1.0, 0.0).astype(bf16)
    carry = _dot3_right(later, tot)
    a_scr[...] = jnp.exp2(z2 - s_incl - carry)

    def v_body(jo, acc):
        j0 = pl.multiple_of(jo * U, U)
        slots = [(base + n_pages + j0 + u) % ns for u in group]
        for u in group:
            wait(slots[u])
        vp = [buf[slots[u]].astype(bf16) for u in group]
        a_rows = a_scr[pl.ds(j0, U), :]
        abd = [jnp.where(own, jnp.broadcast_to(a_rows[u:u + 1, :], (H, R)), 0.0).astype(bf16) for u in group]
        parts = [_dot(abd[u], vp[u]) for u in group]
        for u in group:
            advance(base + n_pages + j0 + u, slots[u])
        return acc + sum(parts[1:], parts[0])

    o_ref[b] = lax.fori_loop(0, n_pages // U, v_body, jnp.zeros((H, HEAD_DIM), f32))


def _dec_attn(page_table, q_s, bias_lanes, msuf, mtot, cache_k, cache_v, n_heads):
    n_batch, n_pages = page_table.shape
    R = cache_k.shape[1]
    assert n_heads == SUBLANES and n_pages % DEC_UNROLL == 0 and DEC_SLOTS % DEC_UNROLL == 0
    assert (2 * n_pages) % DEC_SLOTS == 0
    const2 = lambda b, pt: (0, 0)
    grid_spec = pltpu.PrefetchScalarGridSpec(
        num_scalar_prefetch=1,
        grid=(n_batch,),
        in_specs=[
            pl.BlockSpec((n_batch, n_heads, HEAD_DIM), lambda b, pt: (0, 0, 0)),
            pl.BlockSpec((1, R), const2),
            pl.BlockSpec((R, R), const2),
            pl.BlockSpec((R, R), const2),
            pl.BlockSpec(memory_space=pl.ANY),
            pl.BlockSpec(memory_space=pl.ANY),
        ],
        out_specs=pl.BlockSpec((n_batch, n_heads, HEAD_DIM), lambda b, pt: (0, 0, 0)),
        scratch_shapes=[
            pltpu.VMEM((DEC_SLOTS, R, HEAD_DIM), f32),
            pltpu.SemaphoreType.DMA((DEC_SLOTS,)),
            pltpu.VMEM((n_pages, R), f32),
            pltpu.VMEM((n_pages, R), f32),
        ],
    )
    return pl.pallas_call(
        functools.partial(_dec_attn_kernel, n_heads=n_heads, n_pages=n_pages, n_batch=n_batch,
                          scale=HEAD_DIM ** -0.5),
        grid_spec=grid_spec,
        out_shape=jax.ShapeDtypeStruct((n_batch, n_heads, HEAD_DIM), f32),
        compiler_params=pltpu.CompilerParams(
            dimension_semantics=("arbitrary",), vmem_limit_bytes=VMEM_LIMIT),
    )(page_table, q_s, bias_lanes, msuf, mtot, cache_k, cache_v)


def _dec_finish_kernel(o_ref, q_ref, k_ref, v_ref, bias_ref, gn_ref, out_ref, *, past, dec_seq, scale):
    kpos = past + _iota2(o_ref.shape, 1) * 0 + (dec_seq - 1)
    qpos = past + _iota2(o_ref.shape, 1) * 0 + (dec_seq - 1)
    valid = kpos < qpos
    z = jnp.sum(q_ref[...] * k_ref[...], axis=-1, keepdims=True) * scale + bias_ref[...]
    beta = _sigmoid(z)
    o = o_ref[...]
    o = jnp.where(valid, o * (1.0 - beta) + beta * v_ref[...], o)
    out_ref[...] = o * lax.rsqrt(jnp.mean(o * o, axis=-1, keepdims=True) + EPS) * gn_ref[...]


def _dec_finish(o_past, q_s, k_s, v_s, bias_col, sb_norm, past, dec_seq):
    n = o_past.shape[0]
    full = pl.BlockSpec((n, HEAD_DIM), lambda i: (0, 0))
    return pl.pallas_call(
        functools.partial(_dec_finish_kernel, past=past, dec_seq=dec_seq, scale=HEAD_DIM ** -0.5),
        grid=(1,),
        in_specs=[full, full, full, full, full, pl.BlockSpec((1, HEAD_DIM), lambda i: (0, 0))],
        out_specs=full,
        out_shape=jax.ShapeDtypeStruct((n, HEAD_DIM), f32),
    )(o_past, q_s, k_s, v_s, bias_col, sb_norm)


def _gdn_step_kernel(x_ref, cs_ref, cw_ref, blog_ref, alog_ref, adec_ref, dtb_ref, gn_ref, s_ref,
                     o_ref, s_out_ref, *, n_heads):
    H = n_heads
    x = x_ref[0]
    cs = cs_ref[0]
    y = x[0:3 * H] * cw_ref[CONV_W - 1]
    for i in range(CONV_W - 1):
        y = y + cs[i] * cw_ref[i]
    y = y * _sigmoid(y)
    q = y[0:H]
    k = y[H:2 * H]
    v = y[2 * H:3 * H]
    z = x[3 * H:4 * H]
    q = q * lax.rsqrt(jnp.sum(q * q, axis=-1, keepdims=True) + EPS) * (HEAD_DIM ** -0.5)
    k = k * lax.rsqrt(jnp.sum(k * k, axis=-1, keepdims=True) + EPS)
    beta = _sigmoid(blog_ref[0])
    g = -jnp.exp(alog_ref[...]) * _softplus(adec_ref[0] + dtb_ref[...])
    eg = jnp.exp(g)

    eye = jnp.where(_iota2((HEAD_DIM, HEAD_DIM), 0) == _iota2((HEAD_DIM, HEAD_DIM), 1), 1.0, 0.0).astype(bf16)
    k3 = _split3(k)
    q3 = _split3(q)
    kcol = _dot_nt(eye, k3[0]) + _dot_nt(eye, k3[1]) + _dot_nt(eye, k3[2])
    qcol = _dot_nt(eye, q3[0]) + _dot_nt(eye, q3[1]) + _dot_nt(eye, q3[2])

    outs = []
    for h in range(H):
        s = s_ref[0, h]
        kc = kcol[:, h:h + 1]
        egh = eg[h:h + 1, :]
        ks = jnp.sum(kc * s, axis=0, keepdims=True)
        vn = (v[h:h + 1, :] - egh * ks) * beta[h:h + 1, :]
        s_new = s * egh + kc * vn
        s_out_ref[0, h] = s_new
        outs.append(jnp.sum(qcol[:, h:h + 1] * s_new, axis=0, keepdims=True))
    o = jnp.concatenate(outs, axis=0)
    o = o * lax.rsqrt(jnp.mean(o * o, axis=-1, keepdims=True) + EPS) * gn_ref[...]
    o_ref[0] = o * (z * _sigmoid(z))


def _gdn_step(x_s, conv_state, conv_w, blog, adec, alog, dtb, gdn_norm, state, n_heads):
    n_batch = x_s.shape[0]
    H = n_heads
    const2 = lambda b: (0, 0)
    return pl.pallas_call(
        functools.partial(_gdn_step_kernel, n_heads=n_heads),
        grid=(n_batch,),
        in_specs=[
            pl.BlockSpec((1, 4 * H, HEAD_DIM), lambda b: (b, 0, 0)),
            pl.BlockSpec((1, CONV_W - 1, 3 * H, HEAD_DIM), lambda b: (b, 0, 0, 0)),
            pl.BlockSpec((CONV_W, 3 * H, HEAD_DIM), lambda b: (0, 0, 0)),
            pl.BlockSpec((1, H, LANES), lambda b: (b, 0, 0)),
            pl.BlockSpec((H, LANES), const2),
            pl.BlockSpec((1, H, LANES), lambda b: (b, 0, 0)),
            pl.BlockSpec((H, LANES), const2),
            pl.BlockSpec((1, HEAD_DIM), const2),
            pl.BlockSpec((1, H, HEAD_DIM, HEAD_DIM), lambda b: (b, 0, 0, 0)),
        ],
        out_specs=[
            pl.BlockSpec((1, H, HEAD_DIM), lambda b: (b, 0, 0)),
            pl.BlockSpec((1, H, HEAD_DIM, HEAD_DIM), lambda b: (b, 0, 0, 0)),
        ],
        out_shape=[
            jax.ShapeDtypeStruct((n_batch, H, HEAD_DIM), f32),
            jax.ShapeDtypeStruct((n_batch, H, HEAD_DIM, HEAD_DIM), f32),
        ],
        compiler_params=pltpu.CompilerParams(dimension_semantics=("parallel",)),
    )(x_s, conv_state, conv_w, blog, alog, adec, dtb, gdn_norm, state)


def _place_rows_kernel(a_s_ref, b_s_ref, a_in_ref, b_in_ref, a_ref, b_ref):
    del a_in_ref, b_in_ref
    a_ref[...] = a_s_ref[...]
    b_ref[...] = b_s_ref[...]


def _place_rows(a, b, a_rows, b_rows, row0):
    n, width = a_rows.shape
    rb = SUBLANES
    assert row0 % rb == 0 and n % rb == 0
    small = pl.BlockSpec((rb, width), lambda i: (i, 0))
    big = pl.BlockSpec((rb, width), lambda i: (row0 // rb + i, 0))
    return pl.pallas_call(
        _place_rows_kernel,
        grid=(n // rb,),
        in_specs=[small, small, pl.BlockSpec(memory_space=pl.ANY), pl.BlockSpec(memory_space=pl.ANY)],
        out_specs=[big, big],
        out_shape=[jax.ShapeDtypeStruct(a.shape, a.dtype), jax.ShapeDtypeStruct(b.shape, b.dtype)],
        input_output_aliases={2: 0, 3: 1},
    )(a_rows, b_rows, a, b)


def _outproj_kernel(x_ref, oa_ref, ob_ref, wa_ref, wb_ref, g_ref, x1_ref, hn_ref):
    x1 = x_ref[...] + _dot(oa_ref[...].astype(bf16), wa_ref[...]) + _dot(ob_ref[...].astype(bf16), wb_ref[...])
    x1_ref[...] = x1
    hn = x1 * lax.rsqrt(jnp.mean(x1 * x1, axis=-1, keepdims=True) + EPS) * g_ref[...]
    hn_ref[...] = hn.astype(bf16)


def _outproj(x_all, oa, ob, w_a, w_b, norm_ffn):
    t_pad, d = x_all.shape
    hd = oa.shape[1]
    tm = TM_OUT
    return pl.pallas_call(
        _outproj_kernel,
        grid=(t_pad // tm,),
        in_specs=[
            pl.BlockSpec((tm, d), lambda i: (i, 0)),
            pl.BlockSpec((tm, hd), lambda i: (i, 0)),
            pl.BlockSpec((tm, hd), lambda i: (i, 0)),
            pl.BlockSpec((hd, d), lambda i: (0, 0)),
            pl.BlockSpec((hd, d), lambda i: (0, 0)),
            pl.BlockSpec((1, d), lambda i: (0, 0)),
        ],
        out_specs=[pl.BlockSpec((tm, d), lambda i: (i, 0)), pl.BlockSpec((tm, d), lambda i: (i, 0))],
        out_shape=[jax.ShapeDtypeStruct((t_pad, d), f32), jax.ShapeDtypeStruct((t_pad, d), bf16)],
        compiler_params=pltpu.CompilerParams(
            dimension_semantics=("parallel",), vmem_limit_bytes=VMEM_LIMIT),
    )(x_all, oa, ob, w_a, w_b, norm_ffn)


def _ffn_kernel(x1_ref, hn_ref, wg_ref, wu_ref, wd_ref, y_ref):
    f = pl.program_id(1)

    @pl.when(f == 0)
    def _():
        y_ref[...] = x1_ref[...]

    hn = hn_ref[...]
    gate = _dot(hn, wg_ref[...])
    up = _dot(hn, wu_ref[...])
    act = (gate * _sigmoid(gate) * up).astype(bf16)
    y_ref[...] += _dot(act, wd_ref[...])


def _ffn(x1, hn, w_gate, w_up, w_down):
    t_pad, d = x1.shape
    d_ff = w_gate.shape[1]
    tm, tf = TM_PROJ, TF_FFN
    assert d_ff % tf == 0
    return pl.pallas_call(
        _ffn_kernel,
        grid=(t_pad // tm, d_ff // tf),
        in_specs=[
            pl.BlockSpec((tm, d), lambda i, f: (i, 0)),
            pl.BlockSpec((tm, d), lambda i, f: (i, 0)),
            pl.BlockSpec((d, tf), lambda i, f: (0, f)),
            pl.BlockSpec((d, tf), lambda i, f: (0, f)),
            pl.BlockSpec((tf, d), lambda i, f: (f, 0)),
        ],
        out_specs=pl.BlockSpec((tm, d), lambda i, f: (i, 0)),
        out_shape=jax.ShapeDtypeStruct((t_pad, d), f32),
        compiler_params=pltpu.CompilerParams(
            dimension_semantics=("parallel", "arbitrary"), vmem_limit_bytes=VMEM_LIMIT),
    )(x1, hn, w_gate, w_up, w_down)


def _lane_pad(v, offset):
    return jnp.zeros((1, LANES), f32).at[0, offset:offset + v.shape[0]].set(v)


def kernel(x_prompt, x_sample, cache_k, cache_v, state_gdn, state_conv, page_table, meta, norm_mix, w_in,
           conv_w, a_log, dt_bias, gdn_norm, q_norm, k_norm, sb_norm, sb_bias, w_out, norm_ffn, w_gate,
           w_up, w_down):
    depth = w_in.shape[0]
    bsz, seq, d = x_prompt.shape
    dbsz, dec_seq, _ = x_sample.shape
    assert depth == 1 and bsz == 1 and dec_seq == 1
    H = d // (2 * HEAD_DIM)
    hd = H * HEAD_DIM
    n_pages, page_size = page_table.shape[1], cache_k.shape[2]
    past = n_pages * page_size
    t_valid = N_META + seq
    t_used = t_valid + dbsz
    t_pad = -(-t_used // ROW_ALIGN) * ROW_ALIGN
    l = 0

    x_all = jnp.concatenate(
        [meta.astype(f32), x_prompt[0], x_sample[:, 0], jnp.zeros((t_pad - t_used, d), f32)], axis=0)

    w = w_in[l]
    o_z = 3 * hd
    o_b = o_z + hd
    o_q = o_b + 2 * H
    w_sec = jnp.concatenate([w[:, :o_b], w[:, o_q:]], axis=1).astype(bf16)
    w_ba = jnp.concatenate([w[:, o_b:o_q], jnp.zeros((d, LANES - 2 * H), f32)], axis=1).astype(bf16)

    p, ba, bat = _inproj(x_all, norm_mix[l][None], w_sec, w_ba, w_ba.T, q_norm[l][None], k_norm[l][None], H)

    cw = conv_w[l]
    alog_r = _lane_pad(a_log[l], H)
    dt_r = _lane_pad(dt_bias[l], H)
    oa, s_fin = _gdn_prompt(p, ba, bat, cw, alog_r, dt_r,
                            jnp.broadcast_to(alog_r.T, (LANES, LANES)), jnp.broadcast_to(dt_r.T, (LANES, LANES)),
                            gdn_norm[l][None], H, t_valid)

    ob = _sb_attn_prompt(p, sb_bias[l], sb_norm[l][None], H)

    ps_rows = p[t_valid:t_used]
    ba_s = ba[t_valid:t_used]
    x_s = ps_rows[:, :4 * hd].reshape(dbsz, 4 * H, HEAD_DIM)
    blog = jnp.broadcast_to(ba_s[:, :H, None], (dbsz, H, LANES))
    adec = jnp.broadcast_to(ba_s[:, H:2 * H, None], (dbsz, H, LANES))
    oa_s, s_new = _gdn_step(
        x_s, state_conv[l].reshape(dbsz, CONV_W - 1, 3 * H, HEAD_DIM), cw.reshape(CONV_W, 3 * H, HEAD_DIM),
        blog, adec, jnp.broadcast_to(a_log[l][:, None], (H, LANES)),
        jnp.broadcast_to(dt_bias[l][:, None], (H, LANES)), gdn_norm[l][None], state_gdn[l], H)

    q_s = ps_rows[:, 4 * hd:5 * hd]
    k_s = ps_rows[:, 5 * hd:6 * hd]
    v_s = ps_rows[:, 6 * hd:7 * hd]
    R = page_size * H
    ridx = jnp.arange(R)
    same_head = (ridx[:, None] % H) == (ridx[None, :] % H)
    msuf = jnp.logical_and(same_head, (ridx[:, None] // H) >= (ridx[None, :] // H)).astype(bf16)
    mtot = same_head.astype(bf16)
    bias_lanes = jnp.tile(sb_bias[l], page_size)[None]
    o_past = _dec_attn(page_table, q_s.reshape(dbsz, H, HEAD_DIM), bias_lanes, msuf, mtot,
                       cache_k[l].reshape(-1, R, HEAD_DIM), cache_v[l].reshape(-1, R, HEAD_DIM), H)
    bias_col = jnp.broadcast_to(jnp.tile(sb_bias[l], dbsz)[:, None], (dbsz * H, HEAD_DIM))
    ob_s = _dec_finish(o_past.reshape(dbsz * H, HEAD_DIM), q_s.reshape(dbsz * H, HEAD_DIM),
                       k_s.reshape(dbsz * H, HEAD_DIM), v_s.reshape(dbsz * H, HEAD_DIM),
                       bias_col, sb_norm[l][None], past, dec_seq)

    oa, ob = _place_rows(oa, ob, oa_s.reshape(dbsz, hd), ob_s.reshape(dbsz, hd), t_valid)

    wo = w_out[l].astype(bf16)
    x1, hn = _outproj(x_all, oa, ob, wo[:hd], wo[hd:], norm_ffn[l][None])
    y = _ffn(x1, hn, w_gate[l].astype(bf16), w_up[l].astype(bf16), w_down[l].astype(bf16))

    new_conv_s = jnp.concatenate([state_conv[l][:, 1:], ps_rows[:, None, :3 * hd]], axis=1)
    return (
        y[N_META:t_valid][None],
        y[t_valid:t_used][:, None],
        p[:t_valid, 5 * hd:6 * hd].reshape(1, 1, t_valid, H, HEAD_DIM),
        p[:t_valid, 6 * hd:7 * hd].reshape(1, 1, t_valid, H, HEAD_DIM),
        s_fin[None, None],
        p[t_valid - (CONV_W - 1):t_valid, :3 * hd][None, None],
        k_s.reshape(1, dbsz, 1, H, HEAD_DIM),
        v_s.reshape(1, dbsz, 1, H, HEAD_DIM),
        s_new[None],
        new_conv_s[None],
    )
```

```python
import functools

import jax
import jax.numpy as jnp
from jax import lax
from jax.experimental import pallas as pl
from jax.experimental.pallas import tpu as pltpu

EPS = 1e-6
N_META = 16
HEAD_DIM = 128
CONV_W = 4
LOG2E = 1.4426950408889634
LN2 = 0.6931471805599453

SUBLANES = 8
LANES = 128
VMEM_LIMIT = 56 * 1024 * 1024

ROW_ALIGN = 768
TM_PROJ = 768
TM_OUT = 384
TF_FFN = 512
GDN_CHUNK = 128
TQ_ATT = 768
TK_ATT = 768
ATT_GROUP = 256
ATT_HEADS = 2
DEC_SLOTS = 16
DEC_UNROLL = 8

f32 = jnp.float32
bf16 = jnp.bfloat16


def _dot(a, b):
    return jnp.dot(a, b, preferred_element_type=f32)


def _dot_nt(a, b):
    return lax.dot_general(a, b, (((1,), (1,)), ((), ())), preferred_element_type=f32)


def _dot_tn(a, b):
    return lax.dot_general(a, b, (((0,), (0,)), ((), ())), preferred_element_type=f32)


def _split3(x):
    hi = x.astype(bf16)
    r = x - hi.astype(f32)
    mid = r.astype(bf16)
    lo = (r - mid.astype(f32)).astype(bf16)
    return hi, mid, lo


def _dot3_left(x, m_bf16):
    hi, mid, lo = _split3(x)
    return _dot(hi, m_bf16) + _dot(mid, m_bf16) + _dot(lo, m_bf16)


def _dot3_right(m_bf16, x):
    hi, mid, lo = _split3(x)
    return _dot(m_bf16, hi) + _dot(m_bf16, mid) + _dot(m_bf16, lo)


def _sigmoid(x):
    return 1.0 / (1.0 + jnp.exp(-x))


def _softplus(x):
    return jnp.maximum(x, 0.0) + jnp.log(1.0 + jnp.exp(-jnp.abs(x)))


def _softplus2(z2):
    return jnp.maximum(z2, 0.0) + jnp.log2(1.0 + jnp.exp2(-jnp.abs(z2)))


INV_BASE = 16


def _unit_lower_inverse_minus_eye(mats, ii, jj):
    C = mats[0].shape[0]
    idx = range(len(mats))
    diag = (ii // INV_BASE) == (jj // INV_BASE)
    d = [jnp.where(diag, a, 0.0) for a in mats]
    n = [-x for x in d]
    p = d
    for _ in range(INV_BASE.bit_length() - 2):
        pb = [x.astype(bf16) for x in p]
        p = [_dot(x, x) for x in pb]
        pb = [x.astype(bf16) for x in p]
        n = [n[i] + p[i] + _dot(n[i].astype(bf16), pb[i]) for i in idx]
    bs = INV_BASE
    while bs < C:
        pair = jnp.logical_and((ii // (2 * bs)) == (jj // (2 * bs)), (ii // bs) != (jj // bs))
        off = [jnp.where(pair, a, 0.0) for a in mats]
        x = [off[i] + _dot(off[i].astype(bf16), n[i].astype(bf16)) for i in idx]
        n = [n[i] - (x[i] + _dot(n[i].astype(bf16), x[i].astype(bf16))) for i in idx]
        bs *= 2
    return n


def _iota2(shape, dim):
    return lax.broadcasted_iota(jnp.int32, shape, dim)


def _inproj_kernel(x_ref, g_ref, w_ref, wba_ref, wbat_ref, qn_ref, kn_ref,
                   p_ref, ba_ref, bat_ref, kvb_ref, xn_scr, *, n_heads):
    j = pl.program_id(1)

    @pl.when(j == 0)
    def _():
        x = x_ref[...]
        xn = x * lax.rsqrt(jnp.mean(x * x, axis=-1, keepdims=True) + EPS) * g_ref[...]
        xnb = xn.astype(bf16)
        xn_scr[...] = xnb
        ba_ref[...] = _dot(xnb, wba_ref[...])
        bat_ref[...] = _dot_nt(wbat_ref[...], xnb)

    y = _dot(xn_scr[...], w_ref[...])

    def headnorm(gain_ref, copy_ref):
        for h in range(n_heads):
            sl = slice(h * HEAD_DIM, (h + 1) * HEAD_DIM)
            yh = y[:, sl]
            ms = jnp.mean(yh * yh, axis=-1, keepdims=True)
            yn = yh * lax.rsqrt(ms + EPS) * gain_ref[...]
            p_ref[:, sl] = yn
            if copy_ref is not None:
                copy_ref[:, sl] = yn.astype(bf16)

    @pl.when(j == 4)
    def _():
        headnorm(qn_ref, None)

    @pl.when(j == 5)
    def _():
        headnorm(kn_ref, kvb_ref)

    @pl.when(j == 6)
    def _():
        p_ref[...] = y
        kvb_ref[...] = y.astype(bf16)

    @pl.when(j < 4)
    def _():
        p_ref[...] = y


def _inproj(x_all, norm_mix, w_sec, w_ba, w_bat, q_norm, k_norm, n_heads):
    t_pad, d = x_all.shape
    hd = n_heads * HEAD_DIM
    tm = TM_PROJ
    grid = (t_pad // tm, 7)
    return pl.pallas_call(
        functools.partial(_inproj_kernel, n_heads=n_heads),
        grid=grid,
        in_specs=[
            pl.BlockSpec((tm, d), lambda i, j: (i, 0)),
            pl.BlockSpec((1, d), lambda i, j: (0, 0)),
            pl.BlockSpec((d, hd), lambda i, j: (0, j)),
            pl.BlockSpec((d, LANES), lambda i, j: (0, 0)),
            pl.BlockSpec((LANES, d), lambda i, j: (0, 0)),
            pl.BlockSpec((1, HEAD_DIM), lambda i, j: (0, 0)),
            pl.BlockSpec((1, HEAD_DIM), lambda i, j: (0, 0)),
        ],
        out_specs=[
            pl.BlockSpec((tm, hd), lambda i, j: (i, j)),
            pl.BlockSpec((tm, LANES), lambda i, j: (i, 0)),
            pl.BlockSpec((LANES, tm), lambda i, j: (0, i)),
            pl.BlockSpec((tm, hd), lambda i, j: (i, jnp.clip(j - 5, 0, 1))),
        ],
        out_shape=[
            jax.ShapeDtypeStruct((t_pad, 7 * hd), f32),
            jax.ShapeDtypeStruct((t_pad, LANES), f32),
            jax.ShapeDtypeStruct((LANES, t_pad), f32),
            jax.ShapeDtypeStruct((t_pad, 2 * hd), bf16),
        ],
        scratch_shapes=[pltpu.VMEM((tm, d), bf16)],
        compiler_params=pltpu.CompilerParams(
            dimension_semantics=("parallel", "arbitrary"), vmem_limit_bytes=VMEM_LIMIT),
    )(x_all, norm_mix, w_sec, w_ba, w_bat, q_norm, k_norm)


def _gdn_kernel(x_ref, ba_ref, bat_ref, cw_ref, alog_r_ref, dt_r_ref, alog_c_ref, dt_c_ref, gn_ref,
                oa_ref, s_ref, xbuf, *, n_heads, t_valid):
    c = pl.program_id(0)
    C = GDN_CHUNK
    hd = n_heads * HEAD_DIM
    hist = CONV_W - 1

    @pl.when(c == 0)
    def _():
        xbuf[0:SUBLANES, :] = jnp.zeros((SUBLANES, 3 * hd), f32)
        s_ref[...] = jnp.zeros(s_ref.shape, f32)

    xbuf[SUBLANES:SUBLANES + C, :] = x_ref[:, 0:3 * hd]
    y = xbuf[SUBLANES - hist:SUBLANES - hist + C, :] * cw_ref[0:1, :]
    for i in range(1, CONV_W):
        y = y + xbuf[SUBLANES - hist + i:SUBLANES - hist + i + C, :] * cw_ref[i:i + 1, :]
    y = y * _sigmoid(y)
    xbuf[SUBLANES - hist:SUBLANES, :] = xbuf[SUBLANES + C - hist:SUBLANES + C, :]

    ii = _iota2((C, C), 0)
    jj = _iota2((C, C), 1)
    incl = ii >= jj
    strict = ii > jj
    low_ones = jnp.where(incl, 1.0, 0.0).astype(bf16)
    up_ones = jnp.where(jj >= ii, 1.0, 0.0).astype(bf16)

    row_ok = (c * C + _iota2((C, LANES), 0)) < t_valid
    nr = 2 * n_heads
    lane_ok = (c * C + _iota2((nr, C), 1)) < t_valid
    ba = ba_ref[...]
    bat = bat_ref[0:nr, :]
    beta_cols = jnp.where(row_ok, _sigmoid(ba), 0.0)
    g_cols = jnp.where(row_ok, -jnp.exp(alog_r_ref[...]) * _softplus(ba + dt_r_ref[...]), 0.0)
    g_rows = jnp.where(lane_ok, -jnp.exp(alog_c_ref[0:nr, :]) * _softplus(bat + dt_c_ref[0:nr, :]), 0.0)
    gc_cols = _dot3_right(low_ones, g_cols)
    gc_rows = _dot3_left(g_rows, up_ones)

    heads = range(n_heads)

    def head_cols(off, h):
        return slice(off + h * HEAD_DIM, off + (h + 1) * HEAD_DIM)

    q = [y[:, head_cols(0, h)] for h in heads]
    k = [y[:, head_cols(hd, h)] for h in heads]
    v = [y[:, head_cols(2 * hd, h)] for h in heads]
    q = [x * lax.rsqrt(jnp.sum(x * x, axis=-1, keepdims=True) + EPS) * (HEAD_DIM ** -0.5) for x in q]
    k = [x * lax.rsqrt(jnp.sum(x * x, axis=-1, keepdims=True) + EPS) for x in k]

    beta = [beta_cols[:, h:h + 1] for h in heads]
    gcc = [gc_cols[:, n_heads + h:n_heads + h + 1] for h in heads]
    gcr = [gc_rows[n_heads + h:n_heads + h + 1, :] for h in heads]
    g_last = [gc_cols[C - 1:C, n_heads + h:n_heads + h + 1] for h in heads]
    decay = [jnp.where(incl, jnp.exp(jnp.minimum(gcc[h] - gcr[h], 0.0)), 0.0) for h in heads]
    kb = [k[h] * beta[h] for h in heads]
    kbf = [k[h].astype(bf16) for h in heads]
    a = [jnp.where(strict, _dot_nt(kb[h].astype(bf16), kbf[h]) * decay[h], 0.0) for h in heads]
    qk = [(_dot_nt(q[h].astype(bf16), kbf[h]) * decay[h]).astype(bf16) for h in heads]

    n = _unit_lower_inverse_minus_eye(a, ii, jj)
    egc = [jnp.exp(gcc[h]) for h in heads]
    rhs_u = [v[h] * beta[h] for h in heads]
    rhs_w = [kb[h] * egc[h] for h in heads]
    nb = [n[h].astype(bf16) for h in heads]
    u = [rhs_u[h] + _dot(nb[h], rhs_u[h].astype(bf16)) for h in heads]
    w = [(rhs_w[h] + _dot(nb[h], rhs_w[h].astype(bf16))).astype(bf16) for h in heads]
    q_dec = [(q[h] * egc[h]).astype(bf16) for h in heads]
    k_dec = [(k[h] * jnp.exp(g_last[h] - gcc[h])).astype(bf16) for h in heads]

    s = [s_ref[h] for h in heads]
    sb = [s[h].astype(bf16) for h in heads]
    vnb = [(u[h] - _dot(w[h], sb[h])).astype(bf16) for h in heads]
    o = [_dot(q_dec[h], sb[h]) + _dot(qk[h], vnb[h]) for h in heads]
    for h in heads:
        s_ref[h] = s[h] * jnp.exp(g_last[h]) + _dot_tn(k_dec[h], vnb[h])
    for h in heads:
        z = x_ref[:, head_cols(3 * hd, h)]
        on = o[h] * lax.rsqrt(jnp.mean(o[h] * o[h], axis=-1, keepdims=True) + EPS) * gn_ref[...]
        oa_ref[:, head_cols(0, h)] = on * (z * _sigmoid(z))


def _gdn_prompt(p, ba, bat, conv_w, alog_r, dt_r, alog_c, dt_c, gdn_norm, n_heads, t_valid):
    t_pad = p.shape[0]
    hd = n_heads * HEAD_DIM
    C = GDN_CHUNK
    const = lambda c: (0, 0)
    return pl.pallas_call(
        functools.partial(_gdn_kernel, n_heads=n_heads, t_valid=t_valid),
        grid=(t_pad // C,),
        in_specs=[
            pl.BlockSpec((C, 4 * hd), lambda c: (c, 0)),
            pl.BlockSpec((C, LANES), lambda c: (c, 0)),
            pl.BlockSpec((LANES, C), lambda c: (0, c)),
            pl.BlockSpec((CONV_W, 3 * hd), const),
            pl.BlockSpec((1, LANES), const),
            pl.BlockSpec((1, LANES), const),
            pl.BlockSpec((LANES, LANES), const),
            pl.BlockSpec((LANES, LANES), const),
            pl.BlockSpec((1, HEAD_DIM), const),
        ],
        out_specs=[
            pl.BlockSpec((C, hd), lambda c: (c, 0)),
            pl.BlockSpec((n_heads, HEAD_DIM, HEAD_DIM), lambda c: (0, 0, 0)),
        ],
        out_shape=[
            jax.ShapeDtypeStruct((t_pad, hd), f32),
            jax.ShapeDtypeStruct((n_heads, HEAD_DIM, HEAD_DIM), f32),
        ],
        scratch_shapes=[pltpu.VMEM((SUBLANES + C, 3 * hd), f32)],
        compiler_params=pltpu.CompilerParams(
            dimension_semantics=("arbitrary",), vmem_limit_bytes=VMEM_LIMIT),
    )(p, ba, bat, conv_w, alog_r, dt_r, alog_c, dt_c, gdn_norm)


def _sb_attn_kernel(pt_ref, bias_ref, q_ref, k_ref, v_ref, gn_ref,
                    qd_ref, dbias_ref, msuf_ref, mtot_ref, ck_ref, cv_ref,
                    o_ref, od_ref,
                    acc_scr, carry_scr, buf, sem, z_scr, a_scr, dacc, cnt,
                    *, scale, n_heads, n_pages, n_batch, units_per_span):
    hp = pl.program_id(0)
    i = pl.program_id(1)
    first_step = jnp.logical_and(hp == 0, i == 0)
    last_step = jnp.logical_and(hp == pl.num_programs(0) - 1, i == pl.num_programs(1) - 1)
    dec_prologue, dec_unit, n_units = _decode_units(
        pt_ref, qd_ref, dbias_ref, msuf_ref, mtot_ref, ck_ref, cv_ref, od_ref, buf, sem, z_scr, a_scr, dacc,
        n_heads=n_heads, n_pages=n_pages, n_batch=n_batch, scale=scale)

    @pl.when(first_step)
    def _():
        cnt[0] = 0
        dec_prologue()

    def run_units(n):
        def one(_, c):
            u = cnt[0]

            @pl.when(u < n_units)
            def _():
                dec_unit(u)
                cnt[0] = u + 1

            return c

        lax.fori_loop(0, n, one, 0)

    tq, tk, G = TQ_ATT, TK_ATT, ATT_GROUP
    heads = range(ATT_HEADS)
    groups = range(tk // G)

    def hcols(hh):
        return slice(hh * HEAD_DIM, (hh + 1) * HEAD_DIM)

    lane = _iota2((tq, HEAD_DIM), 1)
    ones3 = jnp.where(lane < 3, 1.0, 0.0).astype(bf16)
    q2, k_tail = [], []
    for hh in heads:
        q2.append(jnp.concatenate([(q_ref[:, hcols(hh)] * (scale * LOG2E)).astype(bf16), ones3], axis=1))
        b_hi, b_mid, b_lo = _split3(jnp.full((tk, HEAD_DIM), bias_ref[hp * ATT_HEADS + hh] * LOG2E, f32))
        klane = _iota2((tk, HEAD_DIM), 1)
        tail = jnp.where(klane == 0, b_hi.astype(f32),
                         jnp.where(klane == 1, b_mid.astype(f32), jnp.where(klane == 2, b_lo.astype(f32), 0.0)))
        k_tail.append(tail.astype(bf16))

    mo = jnp.where(_iota2((G, G), 0) >= _iota2((G, G), 1), 1.0, 0.0).astype(bf16)

    acc_scr[...] = jnp.zeros(acc_scr.shape, f32)
    carry_scr[...] = jnp.zeros(carry_scr.shape, f32)

    def span(start, masked):
        kaug = [jnp.concatenate([k_ref[pl.ds(start, tk), hcols(hh)], k_tail[hh]], axis=1) for hh in heads]
        z2 = [_dot_nt(q2[hh], kaug[hh]) for hh in heads]
        nl = [_softplus2(z) for z in z2]
        if masked:
            valid = _iota2((tq, tk), 1) < _iota2((tq, tk), 0)
            nl = [jnp.where(valid, x, 0.0) for x in nl]
        nlb = [x.astype(bf16) for x in nl]
        carry = [carry_scr[hh, :, 0:1] for hh in heads]
        parts = [[None] * len(groups) for _ in heads]
        for g in reversed(groups):
            gs = slice(g * G, (g + 1) * G)
            for hh in heads:
                sg = _dot(nlb[hh][:, gs], mo)
                w = jnp.exp2(z2[hh][:, gs] - sg - carry[hh])
                if masked:
                    w = jnp.where(valid[:, gs], w, 0.0)
                parts[hh][g] = w.astype(bf16)
                carry[hh] = carry[hh] + sg[:, 0:1]
        for hh in heads:
            carry_scr[hh] = jnp.broadcast_to(carry[hh], (tq, LANES))
            acc_scr[hh] += _dot(jnp.concatenate(parts[hh], axis=-1), v_ref[pl.ds(start, tk), hcols(hh)])

    span(pl.multiple_of(i * tq, tq), True)
    run_units(units_per_span)

    def body(t, _):
        span(pl.multiple_of((i - 1 - t) * tk, tk), False)
        run_units(units_per_span)
        return 0

    lax.fori_loop(0, i * (tq // tk), body, 0)

    for hh in heads:
        o = acc_scr[hh]
        o_ref[:, hcols(hh)] = o * lax.rsqrt(jnp.mean(o * o, axis=-1, keepdims=True) + EPS) * gn_ref[...]

    @pl.when(last_step)
    def _():
        def rest(_, c):
            u = cnt[0]
            dec_unit(u)
            cnt[0] = u + 1
            return c

        lax.fori_loop(0, n_units - cnt[0], rest, 0)


def _sb_attn_prompt_and_decode(p, kvb, sb_bias, sb_norm, page_table, q_dec, dec_bias_lanes, msuf, mtot,
                               cache_k, cache_v, n_heads):
    t_pad = p.shape[0]
    hd = n_heads * HEAD_DIM
    tq = TQ_ATT
    wide = ATT_HEADS * HEAD_DIM
    n_batch, n_pages = page_table.shape
    R = cache_k.shape[1]
    assert TQ_ATT == TK_ATT and TK_ATT % ATT_GROUP == 0 and n_heads % ATT_HEADS == 0
    assert n_heads == SUBLANES and n_pages % DEC_UNROLL == 0 and DEC_SLOTS % DEC_UNROLL == 0
    assert (2 * n_pages) % DEC_SLOTS == 0
    sec = hd // wide
    nq = t_pad // tq
    n_spans = (n_heads // ATT_HEADS) * (nq * (nq + 1) // 2)
    n_units = n_batch * (2 * (n_pages // DEC_UNROLL) + 1)
    units_per_span = -(-n_units // n_spans)
    const2 = lambda h, i, pt: (0, 0)
    grid_spec = pltpu.PrefetchScalarGridSpec(
        num_scalar_prefetch=1,
        grid=(n_heads // ATT_HEADS, nq),
        in_specs=[
            pl.BlockSpec(memory_space=pltpu.SMEM),
            pl.BlockSpec((tq, wide), lambda h, i, pt: (i, 4 * sec + h)),
            pl.BlockSpec((t_pad, wide), lambda h, i, pt: (0, h)),
            pl.BlockSpec((t_pad, wide), lambda h, i, pt: (0, sec + h)),
            pl.BlockSpec((1, HEAD_DIM), const2),
            pl.BlockSpec((n_batch, n_heads, HEAD_DIM), lambda h, i, pt: (0, 0, 0)),
            pl.BlockSpec((1, R), const2),
            pl.BlockSpec((R, R), const2),
            pl.BlockSpec((R, R), const2),
            pl.BlockSpec(memory_space=pl.ANY),
            pl.BlockSpec(memory_space=pl.ANY),
        ],
        out_specs=[
            pl.BlockSpec((tq, wide), lambda h, i, pt: (i, h)),
            pl.BlockSpec((n_batch, n_heads, HEAD_DIM), lambda h, i, pt: (0, 0, 0)),
        ],
        scratch_shapes=[
            pltpu.VMEM((ATT_HEADS, tq, HEAD_DIM), f32),
            pltpu.VMEM((ATT_HEADS, tq, LANES), f32),
            pltpu.VMEM((DEC_SLOTS, R, HEAD_DIM), f32),
            pltpu.SemaphoreType.DMA((DEC_SLOTS,)),
            pltpu.VMEM((n_pages, R), f32),
            pltpu.VMEM((n_pages, R), f32),
            pltpu.VMEM((n_heads, HEAD_DIM), f32),
            pltpu.SMEM((1,), jnp.int32),
        ],
    )
    return pl.pallas_call(
        functools.partial(_sb_attn_kernel, scale=HEAD_DIM ** -0.5, n_heads=n_heads, n_pages=n_pages,
                          n_batch=n_batch, units_per_span=units_per_span),
        grid_spec=grid_spec,
        out_shape=[jax.ShapeDtypeStruct((t_pad, hd), f32),
                   jax.ShapeDtypeStruct((n_batch, n_heads, HEAD_DIM), f32)],
        compiler_params=pltpu.CompilerParams(
            dimension_semantics=("arbitrary", "arbitrary"), vmem_limit_bytes=VMEM_LIMIT),
    )(page_table, sb_bias, p, kvb, kvb, sb_norm, q_dec, dec_bias_lanes, msuf, mtot, cache_k, cache_v)


def _decode_units(pt_ref, q_ref, bias_ref, msuf_ref, mtot_ref, ck_ref, cv_ref, o_ref,
                  buf, sem, z_scr, a_scr, dacc, *, n_heads, n_pages, n_batch, scale):
    H = n_heads
    R = buf.shape[1]
    per_b = 2 * n_pages
    total = n_batch * per_b
    ns = DEC_SLOTS

    def copy(f, slot):
        bb = f // per_b
        jj = f % per_b
        page = pt_ref[bb, jj % n_pages]
        is_k = jj < n_pages
        return is_k, (pltpu.make_async_copy(ck_ref.at[page], buf.at[slot], sem.at[slot]),
                      pltpu.make_async_copy(cv_ref.at[page], buf.at[slot], sem.at[slot]))

    def start(f, slot):
        is_k, (ck, cv) = copy(f, slot)

        @pl.when(is_k)
        def _():
            ck.start()

        @pl.when(jnp.logical_not(is_k))
        def _():
            cv.start()

    def wait(slot):
        pltpu.make_async_copy(ck_ref.at[0], buf.at[slot], sem.at[slot]).wait()

    def prologue():
        for s in range(ns):
            start(s, s)

    def advance(f, slot):
        @pl.when(f + ns < total)
        def _():
            start(f + ns, slot)

    U = DEC_UNROLL
    group = range(U)
    n_groups = n_pages // U
    units_per_b = 2 * n_groups + 1

    def own_lanes():
        return (_iota2((H, R), 1) % H) == _iota2((H, R), 0)

    def key_unit(b, jo):
        qb = (q_ref[b] * (scale * LOG2E)).astype(bf16)
        own = own_lanes()
        j0 = pl.multiple_of(jo * U, U)
        f0 = b * per_b + j0
        slots = [(f0 + u) % ns for u in group]
        for u in group:
            wait(slots[u])
        kp = [buf[slots[u]].astype(bf16) for u in group]
        zz = [_dot_nt(qb, kp[u]) for u in group]
        rows = [jnp.sum(jnp.where(own, zz[u], 0.0), axis=0, keepdims=True) for u in group]
        z_scr[pl.ds(j0, U), :] = jnp.concatenate(rows, axis=0)
        for u in group:
            advance(f0 + u, slots[u])

    def weights_unit():
        z2 = z_scr[...] + bias_ref[...] * LOG2E
        nl = _softplus2(z2)
        nlb = nl.astype(bf16)
        s_incl = _dot(nlb, msuf_ref[...])
        tot = _dot(nlb, mtot_ref[...])
        later = jnp.where(_iota2((n_pages, n_pages), 1) > _iota2((n_pages, n_pages), 0), 1.0, 0.0).astype(bf16)
        carry = _dot3_right(later, tot)
        a_scr[...] = jnp.exp2(z2 - s_incl - carry)
        dacc[...] = jnp.zeros(dacc.shape, f32)

    def value_unit(b, jo):
        own = own_lanes()
        j0 = pl.multiple_of(jo * U, U)
        f0 = b * per_b + n_pages + j0
        slots = [(f0 + u) % ns for u in group]
        for u in group:
            wait(slots[u])
        vp = [buf[slots[u]].astype(bf16) for u in group]
        a_rows = a_scr[pl.ds(j0, U), :]
        abd = [jnp.where(own, jnp.broadcast_to(a_rows[u:u + 1, :], (H, R)), 0.0).astype(bf16) for u in group]
        parts = [_dot(abd[u], vp[u]) for u in group]
        for u in group:
            advance(f0 + u, slots[u])
        dacc[...] += sum(parts[1:], parts[0])

        @pl.when(jo == n_groups - 1)
        def _():
            o_ref[b] = dacc[...]

    def unit(u):
        b = u // units_per_b
        r = u - b * units_per_b

        @pl.when(r < n_groups)
        def _():
            key_unit(b, r)

        @pl.when(r == n_groups)
        def _():
            weights_unit()

        @pl.when(r > n_groups)
        def _():
            value_unit(b, r - (n_groups + 1))

    return prologue, unit, n_batch * units_per_b


def _dec_finish_kernel(o_ref, q_ref, k_ref, v_ref, bias_ref, gn_ref, out_ref, *, past, dec_seq, scale):
    kpos = past + _iota2(o_ref.shape, 1) * 0 + (dec_seq - 1)
    qpos = past + _iota2(o_ref.shape, 1) * 0 + (dec_seq - 1)
    valid = kpos < qpos
    z = jnp.sum(q_ref[...] * k_ref[...], axis=-1, keepdims=True) * scale + bias_ref[...]
    beta = _sigmoid(z)
    o = o_ref[...]
    o = jnp.where(valid, o * (1.0 - beta) + beta * v_ref[...], o)
    out_ref[...] = o * lax.rsqrt(jnp.mean(o * o, axis=-1, keepdims=True) + EPS) * gn_ref[...]


def _dec_finish(o_past, q_s, k_s, v_s, bias_col, sb_norm, past, dec_seq):
    n = o_past.shape[0]
    full = pl.BlockSpec((n, HEAD_DIM), lambda i: (0, 0))
    return pl.pallas_call(
        functools.partial(_dec_finish_kernel, past=past, dec_seq=dec_seq, scale=HEAD_DIM ** -0.5),
        grid=(1,),
        in_specs=[full, full, full, full, full, pl.BlockSpec((1, HEAD_DIM), lambda i: (0, 0))],
        out_specs=full,
        out_shape=jax.ShapeDtypeStruct((n, HEAD_DIM), f32),
    )(o_past, q_s, k_s, v_s, bias_col, sb_norm)


def _gdn_step_kernel(x_ref, cs_ref, cw_ref, blog_ref, alog_ref, adec_ref, dtb_ref, gn_ref, s_ref,
                     o_ref, s_out_ref, *, n_heads):
    H = n_heads
    x = x_ref[0]
    cs = cs_ref[0]
    y = x[0:3 * H] * cw_ref[CONV_W - 1]
    for i in range(CONV_W - 1):
        y = y + cs[i] * cw_ref[i]
    y = y * _sigmoid(y)
    q = y[0:H]
    k = y[H:2 * H]
    v = y[2 * H:3 * H]
    z = x[3 * H:4 * H]
    q = q * lax.rsqrt(jnp.sum(q * q, axis=-1, keepdims=True) + EPS) * (HEAD_DIM ** -0.5)
    k = k * lax.rsqrt(jnp.sum(k * k, axis=-1, keepdims=True) + EPS)
    beta = _sigmoid(blog_ref[0])
    g = -jnp.exp(alog_ref[...]) * _softplus(adec_ref[0] + dtb_ref[...])
    eg = jnp.exp(g)

    eye = jnp.where(_iota2((HEAD_DIM, HEAD_DIM), 0) == _iota2((HEAD_DIM, HEAD_DIM), 1), 1.0, 0.0).astype(bf16)
    k3 = _split3(k)
    q3 = _split3(q)
    kcol = _dot_nt(eye, k3[0]) + _dot_nt(eye, k3[1]) + _dot_nt(eye, k3[2])
    qcol = _dot_nt(eye, q3[0]) + _dot_nt(eye, q3[1]) + _dot_nt(eye, q3[2])

    outs = []
    for h in range(H):
        s = s_ref[0, h]
        kc = kcol[:, h:h + 1]
        egh = eg[h:h + 1, :]
        ks = jnp.sum(kc * s, axis=0, keepdims=True)
        vn = (v[h:h + 1, :] - egh * ks) * beta[h:h + 1, :]
        s_new = s * egh + kc * vn
        s_out_ref[0, h] = s_new
        outs.append(jnp.sum(qcol[:, h:h + 1] * s_new, axis=0, keepdims=True))
    o = jnp.concatenate(outs, axis=0)
    o = o * lax.rsqrt(jnp.mean(o * o, axis=-1, keepdims=True) + EPS) * gn_ref[...]
    o_ref[0] = o * (z * _sigmoid(z))


def _gdn_step(x_s, conv_state, conv_w, blog, adec, alog, dtb, gdn_norm, state, n_heads):
    n_batch = x_s.shape[0]
    H = n_heads
    const2 = lambda b: (0, 0)
    return pl.pallas_call(
        functools.partial(_gdn_step_kernel, n_heads=n_heads),
        grid=(n_batch,),
        in_specs=[
            pl.BlockSpec((1, 4 * H, HEAD_DIM), lambda b: (b, 0, 0)),
            pl.BlockSpec((1, CONV_W - 1, 3 * H, HEAD_DIM), lambda b: (b, 0, 0, 0)),
            pl.BlockSpec((CONV_W, 3 * H, HEAD_DIM), lambda b: (0, 0, 0)),
            pl.BlockSpec((1, H, LANES), lambda b: (b, 0, 0)),
            pl.BlockSpec((H, LANES), const2),
            pl.BlockSpec((1, H, LANES), lambda b: (b, 0, 0)),
            pl.BlockSpec((H, LANES), const2),
            pl.BlockSpec((1, HEAD_DIM), const2),
            pl.BlockSpec((1, H, HEAD_DIM, HEAD_DIM), lambda b: (b, 0, 0, 0)),
        ],
        out_specs=[
            pl.BlockSpec((1, H, HEAD_DIM), lambda b: (b, 0, 0)),
            pl.BlockSpec((1, H, HEAD_DIM, HEAD_DIM), lambda b: (b, 0, 0, 0)),
        ],
        out_shape=[
            jax.ShapeDtypeStruct((n_batch, H, HEAD_DIM), f32),
            jax.ShapeDtypeStruct((n_batch, H, HEAD_DIM, HEAD_DIM), f32),
        ],
        compiler_params=pltpu.CompilerParams(dimension_semantics=("parallel",)),
    )(x_s, conv_state, conv_w, blog, alog, adec, dtb, gdn_norm, state)


def _place_rows_kernel(a_s_ref, b_s_ref, a_in_ref, b_in_ref, a_ref, b_ref):
    del a_in_ref, b_in_ref
    a_ref[...] = a_s_ref[...]
    b_ref[...] = b_s_ref[...]


def _place_rows(a, b, a_rows, b_rows, row0):
    n, width = a_rows.shape
    rb = SUBLANES
    assert row0 % rb == 0 and n % rb == 0
    small = pl.BlockSpec((rb, width), lambda i: (i, 0))
    big = pl.BlockSpec((rb, width), lambda i: (row0 // rb + i, 0))
    return pl.pallas_call(
        _place_rows_kernel,
        grid=(n // rb,),
        in_specs=[small, small, pl.BlockSpec(memory_space=pl.ANY), pl.BlockSpec(memory_space=pl.ANY)],
        out_specs=[big, big],
        out_shape=[jax.ShapeDtypeStruct(a.shape, a.dtype), jax.ShapeDtypeStruct(b.shape, b.dtype)],
        input_output_aliases={2: 0, 3: 1},
    )(a_rows, b_rows, a, b)


def _outproj_kernel(x_ref, oa_ref, ob_ref, wa_ref, wb_ref, g_ref, x1_ref, hn_ref):
    x1 = x_ref[...] + _dot(oa_ref[...].astype(bf16), wa_ref[...]) + _dot(ob_ref[...].astype(bf16), wb_ref[...])
    x1_ref[...] = x1
    hn = x1 * lax.rsqrt(jnp.mean(x1 * x1, axis=-1, keepdims=True) + EPS) * g_ref[...]
    hn_ref[...] = hn.astype(bf16)


def _outproj(x_all, oa, ob, w_a, w_b, norm_ffn):
    t_pad, d = x_all.shape
    hd = oa.shape[1]
    tm = TM_OUT
    return pl.pallas_call(
        _outproj_kernel,
        grid=(t_pad // tm,),
        in_specs=[
            pl.BlockSpec((tm, d), lambda i: (i, 0)),
            pl.BlockSpec((tm, hd), lambda i: (i, 0)),
            pl.BlockSpec((tm, hd), lambda i: (i, 0)),
            pl.BlockSpec((hd, d), lambda i: (0, 0)),
            pl.BlockSpec((hd, d), lambda i: (0, 0)),
            pl.BlockSpec((1, d), lambda i: (0, 0)),
        ],
        out_specs=[pl.BlockSpec((tm, d), lambda i: (i, 0)), pl.BlockSpec((tm, d), lambda i: (i, 0))],
        out_shape=[jax.ShapeDtypeStruct((t_pad, d), f32), jax.ShapeDtypeStruct((t_pad, d), bf16)],
        compiler_params=pltpu.CompilerParams(
            dimension_semantics=("parallel",), vmem_limit_bytes=VMEM_LIMIT),
    )(x_all, oa, ob, w_a, w_b, norm_ffn)


def _ffn_kernel(x1_ref, hn_ref, wg_ref, wu_ref, wd_ref, y_ref):
    f = pl.program_id(1)

    @pl.when(f == 0)
    def _():
        y_ref[...] = x1_ref[...]

    hn = hn_ref[...]
    gate = _dot(hn, wg_ref[...])
    up = _dot(hn, wu_ref[...])
    act = (gate * _sigmoid(gate) * up).astype(bf16)
    y_ref[...] += _dot(act, wd_ref[...])


def _ffn(x1, hn, w_gate, w_up, w_down):
    t_pad, d = x1.shape
    d_ff = w_gate.shape[1]
    tm, tf = TM_PROJ, TF_FFN
    assert d_ff % tf == 0
    return pl.pallas_call(
        _ffn_kernel,
        grid=(t_pad // tm, d_ff // tf),
        in_specs=[
            pl.BlockSpec((tm, d), lambda i, f: (i, 0)),
            pl.BlockSpec((tm, d), lambda i, f: (i, 0)),
            pl.BlockSpec((d, tf), lambda i, f: (0, f)),
            pl.BlockSpec((d, tf), lambda i, f: (0, f)),
            pl.BlockSpec((tf, d), lambda i, f: (f, 0)),
        ],
        out_specs=pl.BlockSpec((tm, d), lambda i, f: (i, 0)),
        out_shape=jax.ShapeDtypeStruct((t_pad, d), f32),
        compiler_params=pltpu.CompilerParams(
            dimension_semantics=("parallel", "arbitrary"), vmem_limit_bytes=VMEM_LIMIT),
    )(x1, hn, w_gate, w_up, w_down)


def _lane_pad(v, offset):
    return jnp.zeros((1, LANES), f32).at[0, offset:offset + v.shape[0]].set(v)


def kernel(x_prompt, x_sample, cache_k, cache_v, state_gdn, state_conv, page_table, meta, norm_mix, w_in,
           conv_w, a_log, dt_bias, gdn_norm, q_norm, k_norm, sb_norm, sb_bias, w_out, norm_ffn, w_gate,
           w_up, w_down):
    depth = w_in.shape[0]
    bsz, seq, d = x_prompt.shape
    dbsz, dec_seq, _ = x_sample.shape
    assert depth == 1 and bsz == 1 and dec_seq == 1
    H = d // (2 * HEAD_DIM)
    hd = H * HEAD_DIM
    n_pages, page_size = page_table.shape[1], cache_k.shape[2]
    past = n_pages * page_size
    t_valid = N_META + seq
    t_used = t_valid + dbsz
    t_pad = -(-t_used // ROW_ALIGN) * ROW_ALIGN
    l = 0

    x_all = jnp.concatenate(
        [meta.astype(f32), x_prompt[0], x_sample[:, 0], jnp.zeros((t_pad - t_used, d), f32)], axis=0)

    w = w_in[l]
    o_z = 3 * hd
    o_b = o_z + hd
    o_q = o_b + 2 * H
    w_sec = jnp.concatenate([w[:, :o_b], w[:, o_q:]], axis=1).astype(bf16)
    w_ba = jnp.concatenate([w[:, o_b:o_q], jnp.zeros((d, LANES - 2 * H), f32)], axis=1).astype(bf16)

    p, ba, bat, kvb = _inproj(x_all, norm_mix[l][None], w_sec, w_ba, w_ba.T, q_norm[l][None], k_norm[l][None], H)

    cw = conv_w[l]
    alog_r = _lane_pad(a_log[l], H)
    dt_r = _lane_pad(dt_bias[l], H)
    oa, s_fin = _gdn_prompt(p, ba, bat, cw, alog_r, dt_r,
                            jnp.broadcast_to(alog_r.T, (LANES, LANES)), jnp.broadcast_to(dt_r.T, (LANES, LANES)),
                            gdn_norm[l][None], H, t_valid)

    ps_rows = p[t_valid:t_used]
    ba_s = ba[t_valid:t_used]
    x_s = ps_rows[:, :4 * hd].reshape(dbsz, 4 * H, HEAD_DIM)
    blog = jnp.broadcast_to(ba_s[:, :H, None], (dbsz, H, LANES))
    adec = jnp.broadcast_to(ba_s[:, H:2 * H, None], (dbsz, H, LANES))
    oa_s, s_new = _gdn_step(
        x_s, state_conv[l].reshape(dbsz, CONV_W - 1, 3 * H, HEAD_DIM), cw.reshape(CONV_W, 3 * H, HEAD_DIM),
        blog, adec, jnp.broadcast_to(a_log[l][:, None], (H, LANES)),
        jnp.broadcast_to(dt_bias[l][:, None], (H, LANES)), gdn_norm[l][None], state_gdn[l], H)

    q_s = ps_rows[:, 4 * hd:5 * hd]
    k_s = ps_rows[:, 5 * hd:6 * hd]
    v_s = ps_rows[:, 6 * hd:7 * hd]
    R = page_size * H
    ridx = jnp.arange(R)
    same_head = (ridx[:, None] % H) == (ridx[None, :] % H)
    msuf = jnp.logical_and(same_head, (ridx[:, None] // H) >= (ridx[None, :] // H)).astype(bf16)
    mtot = same_head.astype(bf16)
    bias_lanes = jnp.tile(sb_bias[l], page_size)[None]
    ob, o_past = _sb_attn_prompt_and_decode(
        p, kvb, sb_bias[l], sb_norm[l][None], page_table, q_s.reshape(dbsz, H, HEAD_DIM), bias_lanes, msuf, mtot,
        cache_k[l].reshape(-1, R, HEAD_DIM), cache_v[l].reshape(-1, R, HEAD_DIM), H)
    bias_col = jnp.broadcast_to(jnp.tile(sb_bias[l], dbsz)[:, None], (dbsz * H, HEAD_DIM))
    ob_s = _dec_finish(o_past.reshape(dbsz * H, HEAD_DIM), q_s.reshape(dbsz * H, HEAD_DIM),
                       k_s.reshape(dbsz * H, HEAD_DIM), v_s.reshape(dbsz * H, HEAD_DIM),
                       bias_col, sb_norm[l][None], past, dec_seq)

    oa, ob = _place_rows(oa, ob, oa_s.reshape(dbsz, hd), ob_s.reshape(dbsz, hd), t_valid)

    wo = w_out[l].astype(bf16)
    x1, hn = _outproj(x_all, oa, ob, wo[:hd], wo[hd:], norm_ffn[l][None])
    y = _ffn(x1, hn, w_gate[l].astype(bf16), w_up[l].astype(bf16), w_down[l].astype(bf16))

    new_conv_s = jnp.concatenate([state_conv[l][:, 1:], ps_rows[:, None, :3 * hd]], axis=1)
    return (
        y[N_META:t_valid][None],
        y[t_valid:t_used][:, None],
        p[:t_valid, 5 * hd:6 * hd].reshape(1, 1, t_valid, H, HEAD_DIM),
        p[:t_valid, 6 * hd:7 * hd].reshape(1, 1, t_valid, H, HEAD_DIM),
        s_fin[None, None],
        p[t_valid - (CONV_W - 1):t_valid, :3 * hd][None, None],
        k_s.reshape(1, dbsz, 1, H, HEAD_DIM),
        v_s.reshape(1, dbsz, 1, H, HEAD_DIM),
        s_new[None],
        new_conv_s[None],
    )
```

```python
import functools

import jax
import jax.numpy as jnp
from jax import lax
from jax.experimental import pallas as pl
from jax.experimental.pallas import tpu as pltpu

EPS = 1e-6
N_META = 16
HEAD_DIM = 128
CONV_W = 4
LOG2E = 1.4426950408889634
LN2 = 0.6931471805599453

SUBLANES = 8
LANES = 128
VMEM_LIMIT = 56 * 1024 * 1024

ROW_ALIGN = 768
TM_PROJ = 768
TM_OUT = 384
TF_FFN = 512
GDN_CHUNK = 128
TQ_ATT = 768
TK_ATT = 768
ATT_GROUP = 256
ATT_HEADS = 2
DEC_SLOTS = 32
DEC_UNROLL = 8

f32 = jnp.float32
bf16 = jnp.bfloat16


def _dot(a, b):
    return jnp.dot(a, b, preferred_element_type=f32)


def _dot_nt(a, b):
    return lax.dot_general(a, b, (((1,), (1,)), ((), ())), preferred_element_type=f32)


def _dot_tn(a, b):
    return lax.dot_general(a, b, (((0,), (0,)), ((), ())), preferred_element_type=f32)


def _split3(x):
    hi = x.astype(bf16)
    r = x - hi.astype(f32)
    mid = r.astype(bf16)
    lo = (r - mid.astype(f32)).astype(bf16)
    return hi, mid, lo


def _dot3_left(x, m_bf16):
    hi, mid, lo = _split3(x)
    return _dot(hi, m_bf16) + _dot(mid, m_bf16) + _dot(lo, m_bf16)


def _dot3_right(m_bf16, x):
    hi, mid, lo = _split3(x)
    return _dot(m_bf16, hi) + _dot(m_bf16, mid) + _dot(m_bf16, lo)


def _sigmoid(x):
    return 1.0 / (1.0 + jnp.exp(-x))


def _softplus(x):
    return jnp.maximum(x, 0.0) + jnp.log(1.0 + jnp.exp(-jnp.abs(x)))


def _softplus2(z2):
    return jnp.maximum(z2, 0.0) + jnp.log2(1.0 + jnp.exp2(-jnp.abs(z2)))


INV_BASE = 16


def _unit_lower_inverse_minus_eye(mats, ii, jj):
    C = mats[0].shape[0]
    idx = range(len(mats))
    diag = (ii // INV_BASE) == (jj // INV_BASE)
    d = [jnp.where(diag, a, 0.0) for a in mats]
    n = [-x for x in d]
    p = d
    for _ in range(INV_BASE.bit_length() - 2):
        pb = [x.astype(bf16) for x in p]
        p = [_dot(x, x) for x in pb]
        pb = [x.astype(bf16) for x in p]
        n = [n[i] + p[i] + _dot(n[i].astype(bf16), pb[i]) for i in idx]
    bs = INV_BASE
    while bs < C:
        pair = jnp.logical_and((ii // (2 * bs)) == (jj // (2 * bs)), (ii // bs) != (jj // bs))
        off = [jnp.where(pair, a, 0.0) for a in mats]
        x = [off[i] + _dot(off[i].astype(bf16), n[i].astype(bf16)) for i in idx]
        n = [n[i] - (x[i] + _dot(n[i].astype(bf16), x[i].astype(bf16))) for i in idx]
        bs *= 2
    return n


def _iota2(shape, dim):
    return lax.broadcasted_iota(jnp.int32, shape, dim)


def _inproj_kernel(x_ref, g_ref, w_ref, wba_ref, wbat_ref, qn_ref, kn_ref,
                   p_ref, ba_ref, bat_ref, kvb_ref, xn_scr, *, n_heads):
    j = pl.program_id(1)

    @pl.when(j == 0)
    def _():
        x = x_ref[...]
        xn = x * lax.rsqrt(jnp.mean(x * x, axis=-1, keepdims=True) + EPS) * g_ref[...]
        xnb = xn.astype(bf16)
        xn_scr[...] = xnb
        ba_ref[...] = _dot(xnb, wba_ref[...])
        bat_ref[...] = _dot_nt(wbat_ref[...], xnb)

    y = _dot(xn_scr[...], w_ref[...])

    def headnorm(gain_ref, copy_ref):
        for h in range(n_heads):
            sl = slice(h * HEAD_DIM, (h + 1) * HEAD_DIM)
            yh = y[:, sl]
            ms = jnp.mean(yh * yh, axis=-1, keepdims=True)
            yn = yh * lax.rsqrt(ms + EPS) * gain_ref[...]
            p_ref[:, sl] = yn
            if copy_ref is not None:
                copy_ref[:, sl] = yn.astype(bf16)

    @pl.when(j == 4)
    def _():
        headnorm(qn_ref, None)

    @pl.when(j == 5)
    def _():
        headnorm(kn_ref, kvb_ref)

    @pl.when(j == 6)
    def _():
        p_ref[...] = y
        kvb_ref[...] = y.astype(bf16)

    @pl.when(j < 4)
    def _():
        p_ref[...] = y


def _inproj(x_all, norm_mix, w_sec, w_ba, w_bat, q_norm, k_norm, n_heads):
    t_pad, d = x_all.shape
    hd = n_heads * HEAD_DIM
    tm = TM_PROJ
    grid = (t_pad // tm, 7)
    return pl.pallas_call(
        functools.partial(_inproj_kernel, n_heads=n_heads),
        grid=grid,
        in_specs=[
            pl.BlockSpec((tm, d), lambda i, j: (i, 0)),
            pl.BlockSpec((1, d), lambda i, j: (0, 0)),
            pl.BlockSpec((d, hd), lambda i, j: (0, j)),
            pl.BlockSpec((d, LANES), lambda i, j: (0, 0)),
            pl.BlockSpec((LANES, d), lambda i, j: (0, 0)),
            pl.BlockSpec((1, HEAD_DIM), lambda i, j: (0, 0)),
            pl.BlockSpec((1, HEAD_DIM), lambda i, j: (0, 0)),
        ],
        out_specs=[
            pl.BlockSpec((tm, hd), lambda i, j: (i, j)),
            pl.BlockSpec((tm, LANES), lambda i, j: (i, 0)),
            pl.BlockSpec((LANES, tm), lambda i, j: (0, i)),
            pl.BlockSpec((tm, hd), lambda i, j: (i, jnp.clip(j - 5, 0, 1))),
        ],
        out_shape=[
            jax.ShapeDtypeStruct((t_pad, 7 * hd), f32),
            jax.ShapeDtypeStruct((t_pad, LANES), f32),
            jax.ShapeDtypeStruct((LANES, t_pad), f32),
            jax.ShapeDtypeStruct((t_pad, 2 * hd), bf16),
        ],
        scratch_shapes=[pltpu.VMEM((tm, d), bf16)],
        compiler_params=pltpu.CompilerParams(
            dimension_semantics=("parallel", "arbitrary"), vmem_limit_bytes=VMEM_LIMIT),
    )(x_all, norm_mix, w_sec, w_ba, w_bat, q_norm, k_norm)


def _gdn_kernel(x_ref, ba_ref, bat_ref, cw_ref, alog_r_ref, dt_r_ref, alog_c_ref, dt_c_ref, gn_ref,
                oa_ref, s_ref, xbuf, *, n_heads, t_valid):
    c = pl.program_id(0)
    C = GDN_CHUNK
    hd = n_heads * HEAD_DIM
    hist = CONV_W - 1

    @pl.when(c == 0)
    def _():
        xbuf[0:SUBLANES, :] = jnp.zeros((SUBLANES, 3 * hd), f32)
        s_ref[...] = jnp.zeros(s_ref.shape, f32)

    xbuf[SUBLANES:SUBLANES + C, :] = x_ref[:, 0:3 * hd]
    y = xbuf[SUBLANES:SUBLANES + C, :] * cw_ref[hist:hist + 1, :]
    for i in range(hist):
        y = y + xbuf[SUBLANES - hist + i:SUBLANES - hist + i + C, :] * cw_ref[i:i + 1, :]
    y = y * _sigmoid(y)
    xbuf[SUBLANES - hist:SUBLANES, :] = xbuf[SUBLANES + C - hist:SUBLANES + C, :]

    ii = _iota2((C, C), 0)
    jj = _iota2((C, C), 1)
    incl = ii >= jj
    strict = ii > jj
    low_ones = jnp.where(incl, 1.0, 0.0).astype(bf16)
    up_ones = jnp.where(jj >= ii, 1.0, 0.0).astype(bf16)

    row_ok = (c * C + _iota2((C, LANES), 0)) < t_valid
    nr = 2 * n_heads
    lane_ok = (c * C + _iota2((nr, C), 1)) < t_valid
    ba = ba_ref[...]
    bat = bat_ref[0:nr, :]
    beta_cols = jnp.where(row_ok, _sigmoid(ba), 0.0)
    g_cols = jnp.where(row_ok, -jnp.exp(alog_r_ref[...]) * _softplus(ba + dt_r_ref[...]), 0.0)
    g_rows = jnp.where(lane_ok, -jnp.exp(alog_c_ref[0:nr, :]) * _softplus(bat + dt_c_ref[0:nr, :]), 0.0)
    gc_cols = _dot3_right(low_ones, g_cols)
    gc_rows = _dot3_left(g_rows, up_ones)

    heads = range(n_heads)

    def head_cols(off, h):
        return slice(off + h * HEAD_DIM, off + (h + 1) * HEAD_DIM)

    q = [y[:, head_cols(0, h)] for h in heads]
    k = [y[:, head_cols(hd, h)] for h in heads]
    v = [y[:, head_cols(2 * hd, h)] for h in heads]
    q = [x * lax.rsqrt(jnp.sum(x * x, axis=-1, keepdims=True) + EPS) * (HEAD_DIM ** -0.5) for x in q]
    k = [x * lax.rsqrt(jnp.sum(x * x, axis=-1, keepdims=True) + EPS) for x in k]

    beta = [beta_cols[:, h:h + 1] for h in heads]
    gcc = [gc_cols[:, n_heads + h:n_heads + h + 1] for h in heads]
    gcr = [gc_rows[n_heads + h:n_heads + h + 1, :] for h in heads]
    g_last = [gc_cols[C - 1:C, n_heads + h:n_heads + h + 1] for h in heads]
    decay = [jnp.where(incl, jnp.exp(jnp.minimum(gcc[h] - gcr[h], 0.0)), 0.0) for h in heads]
    kb = [k[h] * beta[h] for h in heads]
    kbf = [k[h].astype(bf16) for h in heads]
    a = [jnp.where(strict, _dot_nt(kb[h].astype(bf16), kbf[h]) * decay[h], 0.0) for h in heads]
    qk = [(_dot_nt(q[h].astype(bf16), kbf[h]) * decay[h]).astype(bf16) for h in heads]

    n = _unit_lower_inverse_minus_eye(a, ii, jj)
    egc = [jnp.exp(gcc[h]) for h in heads]
    rhs_u = [v[h] * beta[h] for h in heads]
    rhs_w = [kb[h] * egc[h] for h in heads]
    nb = [n[h].astype(bf16) for h in heads]
    u = [rhs_u[h] + _dot(nb[h], rhs_u[h].astype(bf16)) for h in heads]
    w = [(rhs_w[h] + _dot(nb[h], rhs_w[h].astype(bf16))).astype(bf16) for h in heads]
    q_dec = [(q[h] * egc[h]).astype(bf16) for h in heads]
    k_dec = [(k[h] * jnp.exp(g_last[h] - gcc[h])).astype(bf16) for h in heads]

    s = [s_ref[h] for h in heads]
    sb = [s[h].astype(bf16) for h in heads]
    vnb = [(u[h] - _dot(w[h], sb[h])).astype(bf16) for h in heads]
    o = [_dot(q_dec[h], sb[h]) + _dot(qk[h], vnb[h]) for h in heads]
    for h in heads:
        s_ref[h] = s[h] * jnp.exp(g_last[h]) + _dot_tn(k_dec[h], vnb[h])
    for h in heads:
        z = x_ref[:, head_cols(3 * hd, h)]
        on = o[h] * lax.rsqrt(jnp.mean(o[h] * o[h], axis=-1, keepdims=True) + EPS) * gn_ref[...]
        oa_ref[:, head_cols(0, h)] = on * (z * _sigmoid(z))


def _gdn_prompt(p, ba, bat, conv_w, alog_r, dt_r, alog_c, dt_c, gdn_norm, n_heads, t_valid):
    t_pad = p.shape[0]
    hd = n_heads * HEAD_DIM
    C = GDN_CHUNK
    const = lambda c: (0, 0)
    return pl.pallas_call(
        functools.partial(_gdn_kernel, n_heads=n_heads, t_valid=t_valid),
        grid=(t_pad // C,),
        in_specs=[
            pl.BlockSpec((C, 4 * hd), lambda c: (c, 0)),
            pl.BlockSpec((C, LANES), lambda c: (c, 0)),
            pl.BlockSpec((LANES, C), lambda c: (0, c)),
            pl.BlockSpec((CONV_W, 3 * hd), const),
            pl.BlockSpec((1, LANES), const),
            pl.BlockSpec((1, LANES), const),
            pl.BlockSpec((LANES, LANES), const),
            pl.BlockSpec((LANES, LANES), const),
            pl.BlockSpec((1, HEAD_DIM), const),
        ],
        out_specs=[
            pl.BlockSpec((C, hd), lambda c: (c, 0)),
            pl.BlockSpec((n_heads, HEAD_DIM, HEAD_DIM), lambda c: (0, 0, 0)),
        ],
        out_shape=[
            jax.ShapeDtypeStruct((t_pad, hd), f32),
            jax.ShapeDtypeStruct((n_heads, HEAD_DIM, HEAD_DIM), f32),
        ],
        scratch_shapes=[pltpu.VMEM((SUBLANES + C, 3 * hd), f32)],
        compiler_params=pltpu.CompilerParams(
            dimension_semantics=("arbitrary",), vmem_limit_bytes=VMEM_LIMIT),
    )(p, ba, bat, conv_w, alog_r, dt_r, alog_c, dt_c, gdn_norm)


def _sb_attn_kernel(pt_ref, bias_ref, q_ref, k_ref, v_ref, gn_ref,
                    qd_ref, dbias_ref, msuf_ref, mtot_ref, ck_ref, cv_ref,
                    o_ref, od_ref,
                    acc_scr, carry_scr, buf, sem, z_scr, a_scr, dacc, cnt,
                    *, scale, n_heads, n_pages, n_batch, units_per_span):
    hp = pl.program_id(0)
    i = pl.program_id(1)
    first_step = jnp.logical_and(hp == 0, i == 0)
    last_step = jnp.logical_and(hp == pl.num_programs(0) - 1, i == pl.num_programs(1) - 1)
    dec_prologue, dec_unit, n_units = _decode_units(
        pt_ref, qd_ref, dbias_ref, msuf_ref, mtot_ref, ck_ref, cv_ref, od_ref, buf, sem, z_scr, a_scr, dacc,
        n_heads=n_heads, n_pages=n_pages, n_batch=n_batch, scale=scale)

    @pl.when(first_step)
    def _():
        cnt[0] = 0
        dec_prologue()

    def run_units(n):
        def one(_, c):
            u = cnt[0]

            @pl.when(u < n_units)
            def _():
                dec_unit(u)
                cnt[0] = u + 1

            return c

        lax.fori_loop(0, n, one, 0)

    tq, tk, G = TQ_ATT, TK_ATT, ATT_GROUP
    heads = range(ATT_HEADS)
    groups = range(tk // G)

    def hcols(hh):
        return slice(hh * HEAD_DIM, (hh + 1) * HEAD_DIM)

    lane = _iota2((tq, HEAD_DIM), 1)
    ones3 = jnp.where(lane < 3, 1.0, 0.0).astype(bf16)
    q2, k_tail = [], []
    for hh in heads:
        q2.append(jnp.concatenate([(q_ref[:, hcols(hh)] * (scale * LOG2E)).astype(bf16), ones3], axis=1))
        b_hi, b_mid, b_lo = _split3(jnp.full((tk, HEAD_DIM), bias_ref[hp * ATT_HEADS + hh] * LOG2E, f32))
        klane = _iota2((tk, HEAD_DIM), 1)
        tail = jnp.where(klane == 0, b_hi.astype(f32),
                         jnp.where(klane == 1, b_mid.astype(f32), jnp.where(klane == 2, b_lo.astype(f32), 0.0)))
        k_tail.append(tail.astype(bf16))

    mo = jnp.where(_iota2((G, G), 0) >= _iota2((G, G), 1), 1.0, 0.0).astype(bf16)

    acc_scr[...] = jnp.zeros(acc_scr.shape, f32)
    carry_scr[...] = jnp.zeros(carry_scr.shape, f32)

    def span(start, masked):
        kaug = [jnp.concatenate([k_ref[pl.ds(start, tk), hcols(hh)], k_tail[hh]], axis=1) for hh in heads]
        z2 = [_dot_nt(q2[hh], kaug[hh]) for hh in heads]
        nl = [_softplus2(z) for z in z2]
        if masked:
            valid = _iota2((tq, tk), 1) < _iota2((tq, tk), 0)
            nl = [jnp.where(valid, x, 0.0) for x in nl]
        nlb = [x.astype(bf16) for x in nl]
        carry = [carry_scr[hh, :, 0:1] for hh in heads]
        parts = [[None] * len(groups) for _ in heads]
        for g in reversed(groups):
            gs = slice(g * G, (g + 1) * G)
            for hh in heads:
                sg = _dot(nlb[hh][:, gs], mo)
                w = jnp.exp2(z2[hh][:, gs] - sg - carry[hh])
                if masked:
                    w = jnp.where(valid[:, gs], w, 0.0)
                parts[hh][g] = w.astype(bf16)
                carry[hh] = carry[hh] + sg[:, 0:1]
        for hh in heads:
            carry_scr[hh] = jnp.broadcast_to(carry[hh], (tq, LANES))
            acc_scr[hh] += _dot(jnp.concatenate(parts[hh], axis=-1), v_ref[pl.ds(start, tk), hcols(hh)])

    span(pl.multiple_of(i * tq, tq), True)
    run_units(units_per_span)

    def body(t, _):
        span(pl.multiple_of((i - 1 - t) * tk, tk), False)
        run_units(units_per_span)
        return 0

    lax.fori_loop(0, i * (tq // tk), body, 0)

    for hh in heads:
        o = acc_scr[hh]
        o_ref[:, hcols(hh)] = o * lax.rsqrt(jnp.mean(o * o, axis=-1, keepdims=True) + EPS) * gn_ref[...]

    @pl.when(last_step)
    def _():
        def rest(_, c):
            u = cnt[0]
            dec_unit(u)
            cnt[0] = u + 1
            return c

        lax.fori_loop(0, n_units - cnt[0], rest, 0)


def _sb_attn_prompt_and_decode(p, kvb, sb_bias, sb_norm, page_table, q_dec, dec_bias_lanes, msuf, mtot,
                               cache_k, cache_v, n_heads):
    t_pad = p.shape[0]
    hd = n_heads * HEAD_DIM
    tq = TQ_ATT
    wide = ATT_HEADS * HEAD_DIM
    n_batch, n_pages = page_table.shape
    R = cache_k.shape[1]
    assert TQ_ATT == TK_ATT and TK_ATT % ATT_GROUP == 0 and n_heads % ATT_HEADS == 0
    assert n_heads == SUBLANES and n_pages % DEC_UNROLL == 0 and DEC_SLOTS % DEC_UNROLL == 0
    assert (2 * n_pages) % DEC_SLOTS == 0
    sec = hd // wide
    nq = t_pad // tq
    n_spans = (n_heads // ATT_HEADS) * (nq * (nq + 1) // 2)
    n_units = n_batch * (2 * (n_pages // DEC_UNROLL) + 1)
    units_per_span = -(-n_units // n_spans)
    const2 = lambda h, i, pt: (0, 0)
    once = pl.Buffered(1)
    grid_spec = pltpu.PrefetchScalarGridSpec(
        num_scalar_prefetch=1,
        grid=(n_heads // ATT_HEADS, nq),
        in_specs=[
            pl.BlockSpec(memory_space=pltpu.SMEM),
            pl.BlockSpec((tq, wide), lambda h, i, pt: (i, 4 * sec + h)),
            pl.BlockSpec((t_pad, wide), lambda h, i, pt: (0, h), pipeline_mode=once),
            pl.BlockSpec((t_pad, wide), lambda h, i, pt: (0, sec + h), pipeline_mode=once),
            pl.BlockSpec((1, HEAD_DIM), const2),
            pl.BlockSpec((n_batch, n_heads, HEAD_DIM), lambda h, i, pt: (0, 0, 0), pipeline_mode=once),
            pl.BlockSpec((1, R), const2),
            pl.BlockSpec((R, R), const2, pipeline_mode=once),
            pl.BlockSpec((R, R), const2, pipeline_mode=once),
            pl.BlockSpec(memory_space=pl.ANY),
            pl.BlockSpec(memory_space=pl.ANY),
        ],
        out_specs=[
            pl.BlockSpec((tq, wide), lambda h, i, pt: (i, h)),
            pl.BlockSpec((n_batch, n_heads, HEAD_DIM), lambda h, i, pt: (0, 0, 0)),
        ],
        scratch_shapes=[
            pltpu.VMEM((ATT_HEADS, tq, HEAD_DIM), f32),
            pltpu.VMEM((ATT_HEADS, tq, LANES), f32),
            pltpu.VMEM((DEC_SLOTS, R, HEAD_DIM), f32),
            pltpu.SemaphoreType.DMA((DEC_SLOTS,)),
            pltpu.VMEM((n_pages, R), f32),
            pltpu.VMEM((n_pages, R), f32),
            pltpu.VMEM((n_heads, HEAD_DIM), f32),
            pltpu.SMEM((1,), jnp.int32),
        ],
    )
    return pl.pallas_call(
        functools.partial(_sb_attn_kernel, scale=HEAD_DIM ** -0.5, n_heads=n_heads, n_pages=n_pages,
                          n_batch=n_batch, units_per_span=units_per_span),
        grid_spec=grid_spec,
        out_shape=[jax.ShapeDtypeStruct((t_pad, hd), f32),
                   jax.ShapeDtypeStruct((n_batch, n_heads, HEAD_DIM), f32)],
        compiler_params=pltpu.CompilerParams(
            dimension_semantics=("arbitrary", "arbitrary"), vmem_limit_bytes=VMEM_LIMIT),
    )(page_table, sb_bias, p, kvb, kvb, sb_norm, q_dec, dec_bias_lanes, msuf, mtot, cache_k, cache_v)


def _decode_units(pt_ref, q_ref, bias_ref, msuf_ref, mtot_ref, ck_ref, cv_ref, o_ref,
                  buf, sem, z_scr, a_scr, dacc, *, n_heads, n_pages, n_batch, scale):
    H = n_heads
    R = buf.shape[1]
    per_b = 2 * n_pages
    total = n_batch * per_b
    ns = DEC_SLOTS

    def copy(f, slot):
        bb = f // per_b
        jj = f % per_b
        page = pt_ref[bb, jj % n_pages]
        is_k = jj < n_pages
        return is_k, (pltpu.make_async_copy(ck_ref.at[page], buf.at[slot], sem.at[slot]),
                      pltpu.make_async_copy(cv_ref.at[page], buf.at[slot], sem.at[slot]))

    def start(f, slot):
        is_k, (ck, cv) = copy(f, slot)

        @pl.when(is_k)
        def _():
            ck.start()

        @pl.when(jnp.logical_not(is_k))
        def _():
            cv.start()

    def wait(slot):
        pltpu.make_async_copy(ck_ref.at[0], buf.at[slot], sem.at[slot]).wait()

    def prologue():
        for s in range(ns):
            start(s, s)

    def advance(f, slot):
        @pl.when(f + ns < total)
        def _():
            start(f + ns, slot)

    U = DEC_UNROLL
    group = range(U)
    n_groups = n_pages // U
    units_per_b = 2 * n_groups + 1

    def own_lanes():
        return (_iota2((H, R), 1) % H) == _iota2((H, R), 0)

    def key_unit(b, jo):
        qb = (q_ref[b] * (scale * LOG2E)).astype(bf16)
        own = own_lanes()
        j0 = pl.multiple_of(jo * U, U)
        f0 = b * per_b + j0
        slots = [(f0 + u) % ns for u in group]
        for u in group:
            wait(slots[u])
        kp = [buf[slots[u]].astype(bf16) for u in group]
        zz = [_dot_nt(qb, kp[u]) for u in group]
        rows = [jnp.sum(jnp.where(own, zz[u], 0.0), axis=0, keepdims=True) for u in group]
        z_scr[pl.ds(j0, U), :] = jnp.concatenate(rows, axis=0)
        for u in group:
            advance(f0 + u, slots[u])

    def weights_unit():
        z2 = z_scr[...] + bias_ref[...] * LOG2E
        nl = _softplus2(z2)
        nlb = nl.astype(bf16)
        s_incl = _dot(nlb, msuf_ref[...])
        tot = _dot(nlb, mtot_ref[...])
        later = jnp.where(_iota2((n_pages, n_pages), 1) > _iota2((n_pages, n_pages), 0), 1.0, 0.0).astype(bf16)
        carry = _dot3_right(later, tot)
        a_scr[...] = jnp.exp2(z2 - s_incl - carry)
        dacc[...] = jnp.zeros(dacc.shape, f32)

    def value_unit(b, jo):
        own = own_lanes()
        j0 = pl.multiple_of(jo * U, U)
        f0 = b * per_b + n_pages + j0
        slots = [(f0 + u) % ns for u in group]
        for u in group:
            wait(slots[u])
        vp = [buf[slots[u]].astype(bf16) for u in group]
        a_rows = a_scr[pl.ds(j0, U), :]
        abd = [jnp.where(own, jnp.broadcast_to(a_rows[u:u + 1, :], (H, R)), 0.0).astype(bf16) for u in group]
        parts = [_dot(abd[u], vp[u]) for u in group]
        for u in group:
            advance(f0 + u, slots[u])
        dacc[...] += sum(parts[1:], parts[0])

        @pl.when(jo == n_groups - 1)
        def _():
            o_ref[b] = dacc[...]

    def unit(u):
        b = u // units_per_b
        r = u - b * units_per_b

        @pl.when(r < n_groups)
        def _():
            key_unit(b, r)

        @pl.when(r == n_groups)
        def _():
            weights_unit()

        @pl.when(r > n_groups)
        def _():
            value_unit(b, r - (n_groups + 1))

    return prologue, unit, n_batch * units_per_b


def _dec_finish_kernel(o_ref, q_ref, k_ref, v_ref, bias_ref, gn_ref, out_ref, *, past, dec_seq, scale):
    kpos = past + _iota2(o_ref.shape, 1) * 0 + (dec_seq - 1)
    qpos = past + _iota2(o_ref.shape, 1) * 0 + (dec_seq - 1)
    valid = kpos < qpos
    z = jnp.sum(q_ref[...] * k_ref[...], axis=-1, keepdims=True) * scale + bias_ref[...]
    beta = _sigmoid(z)
    o = o_ref[...]
    o = jnp.where(valid, o * (1.0 - beta) + beta * v_ref[...], o)
    out_ref[...] = o * lax.rsqrt(jnp.mean(o * o, axis=-1, keepdims=True) + EPS) * gn_ref[...]


def _dec_finish(o_past, q_s, k_s, v_s, bias_col, sb_norm, past, dec_seq):
    n = o_past.shape[0]
    full = pl.BlockSpec((n, HEAD_DIM), lambda i: (0, 0))
    return pl.pallas_call(
        functools.partial(_dec_finish_kernel, past=past, dec_seq=dec_seq, scale=HEAD_DIM ** -0.5),
        grid=(1,),
        in_specs=[full, full, full, full, full, pl.BlockSpec((1, HEAD_DIM), lambda i: (0, 0))],
        out_specs=full,
        out_shape=jax.ShapeDtypeStruct((n, HEAD_DIM), f32),
    )(o_past, q_s, k_s, v_s, bias_col, sb_norm)


def _gdn_step_kernel(x_ref, cs_ref, cw_ref, blog_ref, alog_ref, adec_ref, dtb_ref, gn_ref, s_ref,
                     o_ref, s_out_ref, *, n_heads):
    H = n_heads
    x = x_ref[0]
    cs = cs_ref[0]
    y = x[0:3 * H] * cw_ref[CONV_W - 1]
    for i in range(CONV_W - 1):
        y = y + cs[i] * cw_ref[i]
    y = y * _sigmoid(y)
    q = y[0:H]
    k = y[H:2 * H]
    v = y[2 * H:3 * H]
    z = x[3 * H:4 * H]
    q = q * lax.rsqrt(jnp.sum(q * q, axis=-1, keepdims=True) + EPS) * (HEAD_DIM ** -0.5)
    k = k * lax.rsqrt(jnp.sum(k * k, axis=-1, keepdims=True) + EPS)
    beta = _sigmoid(blog_ref[0])
    g = -jnp.exp(alog_ref[...]) * _softplus(adec_ref[0] + dtb_ref[...])
    eg = jnp.exp(g)

    eye = jnp.where(_iota2((HEAD_DIM, HEAD_DIM), 0) == _iota2((HEAD_DIM, HEAD_DIM), 1), 1.0, 0.0).astype(bf16)
    k3 = _split3(k)
    q3 = _split3(q)
    kcol = _dot_nt(eye, k3[0]) + _dot_nt(eye, k3[1]) + _dot_nt(eye, k3[2])
    qcol = _dot_nt(eye, q3[0]) + _dot_nt(eye, q3[1]) + _dot_nt(eye, q3[2])

    outs = []
    for h in range(H):
        s = s_ref[0, h]
        kc = kcol[:, h:h + 1]
        egh = eg[h:h + 1, :]
        ks = jnp.sum(kc * s, axis=0, keepdims=True)
        vn = (v[h:h + 1, :] - egh * ks) * beta[h:h + 1, :]
        s_new = s * egh + kc * vn
        s_out_ref[0, h] = s_new
        outs.append(jnp.sum(qcol[:, h:h + 1] * s_new, axis=0, keepdims=True))
    o = jnp.concatenate(outs, axis=0)
    o = o * lax.rsqrt(jnp.mean(o * o, axis=-1, keepdims=True) + EPS) * gn_ref[...]
    o_ref[0] = o * (z * _sigmoid(z))


def _gdn_step(x_s, conv_state, conv_w, blog, adec, alog, dtb, gdn_norm, state, n_heads):
    n_batch = x_s.shape[0]
    H = n_heads
    const2 = lambda b: (0, 0)
    return pl.pallas_call(
        functools.partial(_gdn_step_kernel, n_heads=n_heads),
        grid=(n_batch,),
        in_specs=[
            pl.BlockSpec((1, 4 * H, HEAD_DIM), lambda b: (b, 0, 0)),
            pl.BlockSpec((1, CONV_W - 1, 3 * H, HEAD_DIM), lambda b: (b, 0, 0, 0)),
            pl.BlockSpec((CONV_W, 3 * H, HEAD_DIM), lambda b: (0, 0, 0)),
            pl.BlockSpec((1, H, LANES), lambda b: (b, 0, 0)),
            pl.BlockSpec((H, LANES), const2),
            pl.BlockSpec((1, H, LANES), lambda b: (b, 0, 0)),
            pl.BlockSpec((H, LANES), const2),
            pl.BlockSpec((1, HEAD_DIM), const2),
            pl.BlockSpec((1, H, HEAD_DIM, HEAD_DIM), lambda b: (b, 0, 0, 0)),
        ],
        out_specs=[
            pl.BlockSpec((1, H, HEAD_DIM), lambda b: (b, 0, 0)),
            pl.BlockSpec((1, H, HEAD_DIM, HEAD_DIM), lambda b: (b, 0, 0, 0)),
        ],
        out_shape=[
            jax.ShapeDtypeStruct((n_batch, H, HEAD_DIM), f32),
            jax.ShapeDtypeStruct((n_batch, H, HEAD_DIM, HEAD_DIM), f32),
        ],
        compiler_params=pltpu.CompilerParams(dimension_semantics=("parallel",)),
    )(x_s, conv_state, conv_w, blog, alog, adec, dtb, gdn_norm, state)


def _place_rows_kernel(a_s_ref, b_s_ref, a_in_ref, b_in_ref, a_ref, b_ref):
    del a_in_ref, b_in_ref
    a_ref[...] = a_s_ref[...]
    b_ref[...] = b_s_ref[...]


def _place_rows(a, b, a_rows, b_rows, row0):
    n, width = a_rows.shape
    rb = SUBLANES
    assert row0 % rb == 0 and n % rb == 0
    small = pl.BlockSpec((rb, width), lambda i: (i, 0))
    big = pl.BlockSpec((rb, width), lambda i: (row0 // rb + i, 0))
    return pl.pallas_call(
        _place_rows_kernel,
        grid=(n // rb,),
        in_specs=[small, small, pl.BlockSpec(memory_space=pl.ANY), pl.BlockSpec(memory_space=pl.ANY)],
        out_specs=[big, big],
        out_shape=[jax.ShapeDtypeStruct(a.shape, a.dtype), jax.ShapeDtypeStruct(b.shape, b.dtype)],
        input_output_aliases={2: 0, 3: 1},
    )(a_rows, b_rows, a, b)


def _outproj_kernel(x_ref, oa_ref, ob_ref, wa_ref, wb_ref, g_ref, x1_ref, hn_ref):
    x1 = x_ref[...] + _dot(oa_ref[...].astype(bf16), wa_ref[...]) + _dot(ob_ref[...].astype(bf16), wb_ref[...])
    x1_ref[...] = x1
    hn = x1 * lax.rsqrt(jnp.mean(x1 * x1, axis=-1, keepdims=True) + EPS) * g_ref[...]
    hn_ref[...] = hn.astype(bf16)


def _outproj(x_all, oa, ob, w_a, w_b, norm_ffn):
    t_pad, d = x_all.shape
    hd = oa.shape[1]
    tm = TM_OUT
    return pl.pallas_call(
        _outproj_kernel,
        grid=(t_pad // tm,),
        in_specs=[
            pl.BlockSpec((tm, d), lambda i: (i, 0)),
            pl.BlockSpec((tm, hd), lambda i: (i, 0)),
            pl.BlockSpec((tm, hd), lambda i: (i, 0)),
            pl.BlockSpec((hd, d), lambda i: (0, 0)),
            pl.BlockSpec((hd, d), lambda i: (0, 0)),
            pl.BlockSpec((1, d), lambda i: (0, 0)),
        ],
        out_specs=[pl.BlockSpec((tm, d), lambda i: (i, 0)), pl.BlockSpec((tm, d), lambda i: (i, 0))],
        out_shape=[jax.ShapeDtypeStruct((t_pad, d), f32), jax.ShapeDtypeStruct((t_pad, d), bf16)],
        compiler_params=pltpu.CompilerParams(
            dimension_semantics=("parallel",), vmem_limit_bytes=VMEM_LIMIT),
    )(x_all, oa, ob, w_a, w_b, norm_ffn)


def _ffn_kernel(x1_ref, hn_ref, wg_ref, wu_ref, wd_ref, y_ref):
    f = pl.program_id(1)

    @pl.when(f == 0)
    def _():
        y_ref[...] = x1_ref[...]

    hn = hn_ref[...]
    gate = _dot(hn, wg_ref[...])
    up = _dot(hn, wu_ref[...])
    act = (gate * _sigmoid(gate) * up).astype(bf16)
    y_ref[...] += _dot(act, wd_ref[...])


def _ffn(x1, hn, w_gate, w_up, w_down):
    t_pad, d = x1.shape
    d_ff = w_gate.shape[1]
    tm, tf = TM_PROJ, TF_FFN
    assert d_ff % tf == 0
    return pl.pallas_call(
        _ffn_kernel,
        grid=(t_pad // tm, d_ff // tf),
        in_specs=[
            pl.BlockSpec((tm, d), lambda i, f: (i, 0)),
            pl.BlockSpec((tm, d), lambda i, f: (i, 0)),
            pl.BlockSpec((d, tf), lambda i, f: (0, f)),
            pl.BlockSpec((d, tf), lambda i, f: (0, f)),
            pl.BlockSpec((tf, d), lambda i, f: (f, 0)),
        ],
        out_specs=pl.BlockSpec((tm, d), lambda i, f: (i, 0)),
        out_shape=jax.ShapeDtypeStruct((t_pad, d), f32),
        compiler_params=pltpu.CompilerParams(
            dimension_semantics=("parallel", "arbitrary"), vmem_limit_bytes=VMEM_LIMIT),
    )(x1, hn, w_gate, w_up, w_down)


def _lane_pad(v, offset):
    return jnp.zeros((1, LANES), f32).at[0, offset:offset + v.shape[0]].set(v)


def kernel(x_prompt, x_sample, cache_k, cache_v, state_gdn, state_conv, page_table, meta, norm_mix, w_in,
           conv_w, a_log, dt_bias, gdn_norm, q_norm, k_norm, sb_norm, sb_bias, w_out, norm_ffn, w_gate,
           w_up, w_down):
    depth = w_in.shape[0]
    bsz, seq, d = x_prompt.shape
    dbsz, dec_seq, _ = x_sample.shape
    assert depth == 1 and bsz == 1 and dec_seq == 1
    H = d // (2 * HEAD_DIM)
    hd = H * HEAD_DIM
    n_pages, page_size = page_table.shape[1], cache_k.shape[2]
    past = n_pages * page_size
    t_valid = N_META + seq
    t_used = t_valid + dbsz
    t_pad = -(-t_used // ROW_ALIGN) * ROW_ALIGN
    l = 0

    x_all = jnp.concatenate(
        [meta.astype(f32), x_prompt[0], x_sample[:, 0], jnp.zeros((t_pad - t_used, d), f32)], axis=0)

    w = w_in[l]
    o_z = 3 * hd
    o_b = o_z + hd
    o_q = o_b + 2 * H
    w16 = w.astype(bf16)
    w_sec = jnp.concatenate([w16[:, :o_b], w16[:, o_q:]], axis=1)
    w_ba = jnp.concatenate([w16[:, o_b:o_q], jnp.zeros((d, LANES - 2 * H), bf16)], axis=1)

    p, ba, bat, kvb = _inproj(x_all, norm_mix[l][None], w_sec, w_ba, w_ba.T, q_norm[l][None], k_norm[l][None], H)

    cw = conv_w[l]
    alog_r = _lane_pad(a_log[l], H)
    dt_r = _lane_pad(dt_bias[l], H)
    oa, s_fin = _gdn_prompt(p, ba, bat, cw, alog_r, dt_r,
                            jnp.broadcast_to(alog_r.T, (LANES, LANES)), jnp.broadcast_to(dt_r.T, (LANES, LANES)),
                            gdn_norm[l][None], H, t_valid)

    ps_rows = p[t_valid:t_used]
    ba_s = ba[t_valid:t_used]
    x_s = ps_rows[:, :4 * hd].reshape(dbsz, 4 * H, HEAD_DIM)
    blog = jnp.broadcast_to(ba_s[:, :H, None], (dbsz, H, LANES))
    adec = jnp.broadcast_to(ba_s[:, H:2 * H, None], (dbsz, H, LANES))
    oa_s, s_new = _gdn_step(
        x_s, state_conv[l].reshape(dbsz, CONV_W - 1, 3 * H, HEAD_DIM), cw.reshape(CONV_W, 3 * H, HEAD_DIM),
        blog, adec, jnp.broadcast_to(a_log[l][:, None], (H, LANES)),
        jnp.broadcast_to(dt_bias[l][:, None], (H, LANES)), gdn_norm[l][None], state_gdn[l], H)

    q_s = ps_rows[:, 4 * hd:5 * hd]
    k_s = ps_rows[:, 5 * hd:6 * hd]
    v_s = ps_rows[:, 6 * hd:7 * hd]
    R = page_size * H
    ridx = jnp.arange(R)
    same_head = (ridx[:, None] % H) == (ridx[None, :] % H)
    msuf = jnp.logical_and(same_head, (ridx[:, None] // H) >= (ridx[None, :] // H)).astype(bf16)
    mtot = same_head.astype(bf16)
    bias_lanes = jnp.tile(sb_bias[l], page_size)[None]
    ob, o_past = _sb_attn_prompt_and_decode(
        p, kvb, sb_bias[l], sb_norm[l][None], page_table, q_s.reshape(dbsz, H, HEAD_DIM), bias_lanes, msuf, mtot,
        cache_k[l].reshape(-1, R, HEAD_DIM), cache_v[l].reshape(-1, R, HEAD_DIM), H)
    bias_col = jnp.broadcast_to(jnp.tile(sb_bias[l], dbsz)[:, None], (dbsz * H, HEAD_DIM))
    ob_s = _dec_finish(o_past.reshape(dbsz * H, HEAD_DIM), q_s.reshape(dbsz * H, HEAD_DIM),
                       k_s.reshape(dbsz * H, HEAD_DIM), v_s.reshape(dbsz * H, HEAD_DIM),
                       bias_col, sb_norm[l][None], past, dec_seq)

    oa, ob = _place_rows(oa, ob, oa_s.reshape(dbsz, hd), ob_s.reshape(dbsz, hd), t_valid)

    wo = w_out[l].astype(bf16)
    x1, hn = _outproj(x_all, oa, ob, wo[:hd], wo[hd:], norm_ffn[l][None])
    y = _ffn(x1, hn, w_gate[l].astype(bf16), w_up[l].astype(bf16), w_down[l].astype(bf16))

    new_conv_s = jnp.concatenate([state_conv[l][:, 1:], ps_rows[:, None, :3 * hd]], axis=1)
    return (
        y[N_META:t_valid][None],
        y[t_valid:t_used][:, None],
        p[:t_valid, 5 * hd:6 * hd].reshape(1, 1, t_valid, H, HEAD_DIM),
        p[:t_valid, 6 * hd:7 * hd].reshape(1, 1, t_valid, H, HEAD_DIM),
        s_fin[None, None],
        p[t_valid - (CONV_W - 1):t_valid, :3 * hd][None, None],
        k_s.reshape(1, dbsz, 1, H, HEAD_DIM),
        v_s.reshape(1, dbsz, 1, H, HEAD_DIM),
        s_new[None],
        new_conv_s[None],
    )
```

```python
import functools
import types

import jax
import jax.numpy as jnp
from jax import lax
from jax.experimental import pallas as pl
from jax.experimental.pallas import tpu as pltpu

EPS = 1e-6
N_META = 16
HEAD_DIM = 128
CONV_W = 4
LOG2E = 1.4426950408889634
LN2 = 0.6931471805599453

SUBLANES = 8
LANES = 128
VMEM_LIMIT = 56 * 1024 * 1024

ROW_ALIGN = 768
TM_PROJ = 768
TM_OUT = 384
TF_FFN = 512
GDN_CHUNK = 128
TQ_ATT = 768
TK_ATT = 768
ATT_GROUP = 256
ATT_HEADS = 2
DEC_SLOTS = 32
DEC_UNROLL = 8

f32 = jnp.float32
bf16 = jnp.bfloat16


def _dot(a, b):
    return jnp.dot(a, b, preferred_element_type=f32)


def _dot_nt(a, b):
    return lax.dot_general(a, b, (((1,), (1,)), ((), ())), preferred_element_type=f32)


def _dot_tn(a, b):
    return lax.dot_general(a, b, (((0,), (0,)), ((), ())), preferred_element_type=f32)


def _split3(x):
    hi = x.astype(bf16)
    r = x - hi.astype(f32)
    mid = r.astype(bf16)
    lo = (r - mid.astype(f32)).astype(bf16)
    return hi, mid, lo


def _dot3_left(x, m_bf16):
    hi, mid, lo = _split3(x)
    return _dot(hi, m_bf16) + _dot(mid, m_bf16) + _dot(lo, m_bf16)


def _dot3_right(m_bf16, x):
    hi, mid, lo = _split3(x)
    return _dot(m_bf16, hi) + _dot(m_bf16, mid) + _dot(m_bf16, lo)


def _sigmoid(x):
    return 1.0 / (1.0 + jnp.exp(-x))


def _softplus(x):
    return jnp.maximum(x, 0.0) + jnp.log(1.0 + jnp.exp(-jnp.abs(x)))


def _softplus2(z2):
    return jnp.maximum(z2, 0.0) + jnp.log2(1.0 + jnp.exp2(-jnp.abs(z2)))


INV_BASE = 16


def _unit_lower_inverse_minus_eye(mats, ii, jj):
    C = mats[0].shape[0]
    idx = range(len(mats))
    diag = (ii // INV_BASE) == (jj // INV_BASE)
    d = [jnp.where(diag, a, 0.0) for a in mats]
    n = [-x for x in d]
    p = d
    for _ in range(INV_BASE.bit_length() - 2):
        pb = [x.astype(bf16) for x in p]
        p = [_dot(x, x) for x in pb]
        pb = [x.astype(bf16) for x in p]
        n = [n[i] + p[i] + _dot(n[i].astype(bf16), pb[i]) for i in idx]
    bs = INV_BASE
    while bs < C:
        pair = jnp.logical_and((ii // (2 * bs)) == (jj // (2 * bs)), (ii // bs) != (jj // bs))
        off = [jnp.where(pair, a, 0.0) for a in mats]
        x = [off[i] + _dot(off[i].astype(bf16), n[i].astype(bf16)) for i in idx]
        n = [n[i] - (x[i] + _dot(n[i].astype(bf16), x[i].astype(bf16))) for i in idx]
        bs *= 2
    return n


def _iota2(shape, dim):
    return lax.broadcasted_iota(jnp.int32, shape, dim)


def _inproj_kernel(x_ref, g_ref, w_ref, wba_ref, wbat_ref, qn_ref, kn_ref,
                   p_ref, ba_ref, bat_ref, kvb_ref, xn_scr, *, n_heads):
    j = pl.program_id(1)

    @pl.when(j == 0)
    def _():
        x = x_ref[...]
        xn = x * lax.rsqrt(jnp.mean(x * x, axis=-1, keepdims=True) + EPS) * g_ref[...]
        xnb = xn.astype(bf16)
        xn_scr[...] = xnb
        ba_ref[...] = _dot(xnb, wba_ref[...])
        bat_ref[...] = _dot_nt(wbat_ref[...], xnb)

    y = _dot(xn_scr[...], w_ref[...])

    def headnorm(gain_ref, copy_ref):
        for h in range(n_heads):
            sl = slice(h * HEAD_DIM, (h + 1) * HEAD_DIM)
            yh = y[:, sl]
            ms = jnp.mean(yh * yh, axis=-1, keepdims=True)
            yn = yh * lax.rsqrt(ms + EPS) * gain_ref[...]
            p_ref[:, sl] = yn
            if copy_ref is not None:
                copy_ref[:, sl] = yn.astype(bf16)

    @pl.when(j == 4)
    def _():
        headnorm(qn_ref, None)

    @pl.when(j == 5)
    def _():
        headnorm(kn_ref, kvb_ref)

    @pl.when(j == 6)
    def _():
        p_ref[...] = y
        kvb_ref[...] = y.astype(bf16)

    @pl.when(j < 4)
    def _():
        p_ref[...] = y


def _inproj(x_all, norm_mix, w_sec, w_ba, w_bat, q_norm, k_norm, n_heads):
    t_pad, d = x_all.shape
    hd = n_heads * HEAD_DIM
    tm = TM_PROJ
    grid = (t_pad // tm, 7)
    return pl.pallas_call(
        functools.partial(_inproj_kernel, n_heads=n_heads),
        grid=grid,
        in_specs=[
            pl.BlockSpec((tm, d), lambda i, j: (i, 0)),
            pl.BlockSpec((1, d), lambda i, j: (0, 0)),
            pl.BlockSpec((d, hd), lambda i, j: (0, j)),
            pl.BlockSpec((d, LANES), lambda i, j: (0, 0)),
            pl.BlockSpec((LANES, d), lambda i, j: (0, 0)),
            pl.BlockSpec((1, HEAD_DIM), lambda i, j: (0, 0)),
            pl.BlockSpec((1, HEAD_DIM), lambda i, j: (0, 0)),
        ],
        out_specs=[
            pl.BlockSpec((tm, hd), lambda i, j: (i, j)),
            pl.BlockSpec((tm, LANES), lambda i, j: (i, 0)),
            pl.BlockSpec((LANES, tm), lambda i, j: (0, i)),
            pl.BlockSpec((tm, hd), lambda i, j: (i, jnp.clip(j - 5, 0, 1))),
        ],
        out_shape=[
            jax.ShapeDtypeStruct((t_pad, 7 * hd), f32),
            jax.ShapeDtypeStruct((t_pad, LANES), f32),
            jax.ShapeDtypeStruct((LANES, t_pad), f32),
            jax.ShapeDtypeStruct((t_pad, 2 * hd), bf16),
        ],
        scratch_shapes=[pltpu.VMEM((tm, d), bf16)],
        compiler_params=pltpu.CompilerParams(
            dimension_semantics=("parallel", "arbitrary"), vmem_limit_bytes=VMEM_LIMIT),
    )(x_all, norm_mix, w_sec, w_ba, w_bat, q_norm, k_norm)


def _gdn_kernel(x_ref, ba_ref, bat_ref, cw_ref, alog_r_ref, dt_r_ref, alog_c_ref, dt_c_ref, gn_ref,
                oa_ref, s_ref, xbuf, *, n_heads, t_valid):
    c = pl.program_id(0)
    C = GDN_CHUNK
    hd = n_heads * HEAD_DIM
    hist = CONV_W - 1

    @pl.when(c == 0)
    def _():
        xbuf[0:SUBLANES, :] = jnp.zeros((SUBLANES, 3 * hd), f32)
        s_ref[...] = jnp.zeros(s_ref.shape, f32)

    xbuf[SUBLANES:SUBLANES + C, :] = x_ref[:, 0:3 * hd]
    y = xbuf[SUBLANES:SUBLANES + C, :] * cw_ref[hist:hist + 1, :]
    for i in range(hist):
        y = y + xbuf[SUBLANES - hist + i:SUBLANES - hist + i + C, :] * cw_ref[i:i + 1, :]
    y = y * _sigmoid(y)
    xbuf[SUBLANES - hist:SUBLANES, :] = xbuf[SUBLANES + C - hist:SUBLANES + C, :]

    ii = _iota2((C, C), 0)
    jj = _iota2((C, C), 1)
    incl = ii >= jj
    strict = ii > jj
    low_ones = jnp.where(incl, 1.0, 0.0).astype(bf16)
    up_ones = jnp.where(jj >= ii, 1.0, 0.0).astype(bf16)

    row_ok = (c * C + _iota2((C, LANES), 0)) < t_valid
    nr = 2 * n_heads
    lane_ok = (c * C + _iota2((nr, C), 1)) < t_valid
    ba = ba_ref[...]
    bat = bat_ref[0:nr, :]
    beta_cols = jnp.where(row_ok, _sigmoid(ba), 0.0)
    g_cols = jnp.where(row_ok, -jnp.exp(alog_r_ref[...]) * _softplus(ba + dt_r_ref[...]), 0.0)
    g_rows = jnp.where(lane_ok, -jnp.exp(alog_c_ref[0:nr, :]) * _softplus(bat + dt_c_ref[0:nr, :]), 0.0)
    gc_cols = _dot3_right(low_ones, g_cols)
    gc_rows = _dot3_left(g_rows, up_ones)

    heads = range(n_heads)

    def head_cols(off, h):
        return slice(off + h * HEAD_DIM, off + (h + 1) * HEAD_DIM)

    q = [y[:, head_cols(0, h)] for h in heads]
    k = [y[:, head_cols(hd, h)] for h in heads]
    v = [y[:, head_cols(2 * hd, h)] for h in heads]
    q = [x * lax.rsqrt(jnp.sum(x * x, axis=-1, keepdims=True) + EPS) * (HEAD_DIM ** -0.5) for x in q]
    k = [x * lax.rsqrt(jnp.sum(x * x, axis=-1, keepdims=True) + EPS) for x in k]

    beta = [beta_cols[:, h:h + 1] for h in heads]
    gcc = [gc_cols[:, n_heads + h:n_heads + h + 1] for h in heads]
    gcr = [gc_rows[n_heads + h:n_heads + h + 1, :] for h in heads]
    g_last = [gc_cols[C - 1:C, n_heads + h:n_heads + h + 1] for h in heads]
    decay = [jnp.where(incl, jnp.exp(jnp.minimum(gcc[h] - gcr[h], 0.0)), 0.0) for h in heads]
    kb = [k[h] * beta[h] for h in heads]
    kbf = [k[h].astype(bf16) for h in heads]
    a = [jnp.where(strict, _dot_nt(kb[h].astype(bf16), kbf[h]) * decay[h], 0.0) for h in heads]
    qk = [(_dot_nt(q[h].astype(bf16), kbf[h]) * decay[h]).astype(bf16) for h in heads]

    n = _unit_lower_inverse_minus_eye(a, ii, jj)
    egc = [jnp.exp(gcc[h]) for h in heads]
    rhs_u = [v[h] * beta[h] for h in heads]
    rhs_w = [kb[h] * egc[h] for h in heads]
    nb = [n[h].astype(bf16) for h in heads]
    u = [rhs_u[h] + _dot(nb[h], rhs_u[h].astype(bf16)) for h in heads]
    w = [(rhs_w[h] + _dot(nb[h], rhs_w[h].astype(bf16))).astype(bf16) for h in heads]
    q_dec = [(q[h] * egc[h]).astype(bf16) for h in heads]
    k_dec = [(k[h] * jnp.exp(g_last[h] - gcc[h])).astype(bf16) for h in heads]

    s = [s_ref[h] for h in heads]
    sb = [s[h].astype(bf16) for h in heads]
    vnb = [(u[h] - _dot(w[h], sb[h])).astype(bf16) for h in heads]
    o = [_dot(q_dec[h], sb[h]) + _dot(qk[h], vnb[h]) for h in heads]
    for h in heads:
        s_ref[h] = s[h] * jnp.exp(g_last[h]) + _dot_tn(k_dec[h], vnb[h])
    for h in heads:
        z = x_ref[:, head_cols(3 * hd, h)]
        on = o[h] * lax.rsqrt(jnp.mean(o[h] * o[h], axis=-1, keepdims=True) + EPS) * gn_ref[...]
        oa_ref[:, head_cols(0, h)] = on * (z * _sigmoid(z))


def _gdn_prompt(p, ba, bat, conv_w, alog_r, dt_r, alog_c, dt_c, gdn_norm, n_heads, t_valid):
    t_pad = p.shape[0]
    hd = n_heads * HEAD_DIM
    C = GDN_CHUNK
    const = lambda c: (0, 0)
    return pl.pallas_call(
        functools.partial(_gdn_kernel, n_heads=n_heads, t_valid=t_valid),
        grid=(t_pad // C,),
        in_specs=[
            pl.BlockSpec((C, 4 * hd), lambda c: (c, 0)),
            pl.BlockSpec((C, LANES), lambda c: (c, 0)),
            pl.BlockSpec((LANES, C), lambda c: (0, c)),
            pl.BlockSpec((CONV_W, 3 * hd), const),
            pl.BlockSpec((1, LANES), const),
            pl.BlockSpec((1, LANES), const),
            pl.BlockSpec((LANES, LANES), const),
            pl.BlockSpec((LANES, LANES), const),
            pl.BlockSpec((1, HEAD_DIM), const),
        ],
        out_specs=[
            pl.BlockSpec((C, hd), lambda c: (c, 0)),
            pl.BlockSpec((n_heads, HEAD_DIM, HEAD_DIM), lambda c: (0, 0, 0)),
        ],
        out_shape=[
            jax.ShapeDtypeStruct((t_pad, hd), f32),
            jax.ShapeDtypeStruct((n_heads, HEAD_DIM, HEAD_DIM), f32),
        ],
        scratch_shapes=[pltpu.VMEM((SUBLANES + C, 3 * hd), f32)],
        compiler_params=pltpu.CompilerParams(
            dimension_semantics=("arbitrary",), vmem_limit_bytes=VMEM_LIMIT),
    )(p, ba, bat, conv_w, alog_r, dt_r, alog_c, dt_c, gdn_norm)


def _sb_attn_kernel(pt_ref, bias_ref, q_ref, k_ref, v_ref, gn_ref,
                    qd_ref, dbias_ref, msuf_ref, mtot_ref, ck_ref, cv_ref,
                    o_ref, od_ref,
                    acc_scr, carry_scr, buf, sem, z_scr, a_scr, dacc, cnt,
                    *, scale, n_heads, n_pages, n_batch, n_units):
    hp = pl.program_id(0)
    i = pl.program_id(1)
    first_step = jnp.logical_and(hp == 0, i == 0)
    dec = _decode_stream(
        pt_ref, qd_ref, dbias_ref, msuf_ref, mtot_ref, ck_ref, cv_ref, od_ref, buf, sem, z_scr, a_scr, dacc,
        n_heads=n_heads, n_pages=n_pages, n_batch=n_batch, n_units=n_units, scale=scale)

    @pl.when(first_step)
    def _():
        cnt[0] = 0
        dec.prologue()

    tq, tk, G = TQ_ATT, TK_ATT, ATT_GROUP
    heads = range(ATT_HEADS)
    groups = range(tk // G)

    def hcols(hh):
        return slice(hh * HEAD_DIM, (hh + 1) * HEAD_DIM)

    lane = _iota2((tq, HEAD_DIM), 1)
    ones3 = jnp.where(lane < 3, 1.0, 0.0).astype(bf16)
    q2, k_tail = [], []
    for hh in heads:
        q2.append(jnp.concatenate([(q_ref[:, hcols(hh)] * (scale * LOG2E)).astype(bf16), ones3], axis=1))
        b_hi, b_mid, b_lo = _split3(jnp.full((tk, HEAD_DIM), bias_ref[hp * ATT_HEADS + hh] * LOG2E, f32))
        klane = _iota2((tk, HEAD_DIM), 1)
        tail = jnp.where(klane == 0, b_hi.astype(f32),
                         jnp.where(klane == 1, b_mid.astype(f32), jnp.where(klane == 2, b_lo.astype(f32), 0.0)))
        k_tail.append(tail.astype(bf16))

    mo = jnp.where(_iota2((G, G), 0) >= _iota2((G, G), 1), 1.0, 0.0).astype(bf16)

    acc_scr[...] = jnp.zeros(acc_scr.shape, f32)
    carry_scr[...] = jnp.zeros(carry_scr.shape, f32)

    def span(start, masked):
        u0 = cnt[0]
        npg = 2 * DEC_UNROLL
        slices = [(hh, g) for g in groups for hh in heads]

        def unit_pieces(u):
            rows, sums = [], []
            todo = [lambda t=t: rows.append(dec.key_page(u, t)) for t in range(DEC_UNROLL)]
            todo += [lambda t=t: sums.append(dec.value_page(u, t)) for t in range(DEC_UNROLL)]
            return todo, rows, sums

        def run_interleaved(att, todo):
            done = 0
            for n, piece in enumerate(att):
                piece()
                upto = (n + 1) * len(todo) // len(att)
                for t in range(done, upto):
                    todo[t]()
                done = upto

        dec.wait(u0)
        kaug = [jnp.concatenate([k_ref[pl.ds(start, tk), hcols(hh)], k_tail[hh]], axis=1) for hh in heads]
        z2 = [[None] * len(groups) for _ in heads]
        nlb = [[None] * len(groups) for _ in heads]
        valid = (_iota2((tq, tk), 1) < _iota2((tq, tk), 0)) if masked else None

        def logits(hh, g):
            gs = slice(g * G, (g + 1) * G)
            z = _dot_nt(q2[hh], kaug[hh][gs, :])
            nl = _softplus2(z)
            if masked:
                nl = jnp.where(valid[:, gs], nl, 0.0)
            z2[hh][g] = z
            nlb[hh][g] = nl.astype(bf16)

        todo, rows, sums = unit_pieces(u0)
        run_interleaved([lambda hh=hh, g=g: logits(hh, g) for hh, g in slices], todo)
        dec.key_store(u0, rows)
        dec.finish(u0, sum(sums[1:], sums[0]))

        u1 = u0 + 1
        dec.wait(u1)
        carry = [carry_scr[hh, :, 0:1] for hh in heads]
        parts = [[None] * len(groups) for _ in heads]

        def weights(hh, g):
            gs = slice(g * G, (g + 1) * G)
            sg = _dot(nlb[hh][g], mo)
            w = jnp.exp2(z2[hh][g] - sg - carry[hh])
            if masked:
                w = jnp.where(valid[:, gs], w, 0.0)
            parts[hh][g] = w.astype(bf16)
            carry[hh] = carry[hh] + sg[:, 0:1]

        todo, rows, sums = unit_pieces(u1)
        run_interleaved([lambda hh=hh, g=g: weights(hh, g) for hh, g in reversed(slices)], todo)
        dec.key_store(u1, rows)
        for hh in heads:
            carry_scr[hh] = jnp.broadcast_to(carry[hh], (tq, LANES))
            acc_scr[hh] += _dot(jnp.concatenate(parts[hh], axis=-1), v_ref[pl.ds(start, tk), hcols(hh)])
        dec.finish(u1, sum(sums[1:], sums[0]))
        cnt[0] = u0 + 2

    span(pl.multiple_of(i * tq, tq), True)

    def body(t, _):
        span(pl.multiple_of((i - 1 - t) * tk, tk), False)
        return 0

    lax.fori_loop(0, i * (tq // tk), body, 0)

    for hh in heads:
        o = acc_scr[hh]
        o_ref[:, hcols(hh)] = o * lax.rsqrt(jnp.mean(o * o, axis=-1, keepdims=True) + EPS) * gn_ref[...]


def _sb_attn_prompt_and_decode(p, kvb, sb_bias, sb_norm, page_table, q_dec, dec_bias_lanes, msuf, mtot,
                               cache_k, cache_v, n_heads):
    t_pad = p.shape[0]
    hd = n_heads * HEAD_DIM
    tq = TQ_ATT
    wide = ATT_HEADS * HEAD_DIM
    n_batch, n_pages = page_table.shape
    R = cache_k.shape[1]
    assert TQ_ATT == TK_ATT and TK_ATT % ATT_GROUP == 0 and n_heads % ATT_HEADS == 0
    assert n_heads == SUBLANES and n_pages % DEC_UNROLL == 0 and DEC_SLOTS % DEC_UNROLL == 0
    assert (2 * n_pages) % DEC_SLOTS == 0
    sec = hd // wide
    nq = t_pad // tq
    n_spans = (n_heads // ATT_HEADS) * (nq * (nq + 1) // 2)
    n_units = 2 * n_spans
    assert n_units >= (n_batch + 1) * (n_pages // DEC_UNROLL), "more page stream than attention to hide it under"
    assert tq // ATT_GROUP == 3
    const2 = lambda h, i, pt: (0, 0)
    once = pl.Buffered(1)
    grid_spec = pltpu.PrefetchScalarGridSpec(
        num_scalar_prefetch=1,
        grid=(n_heads // ATT_HEADS, nq),
        in_specs=[
            pl.BlockSpec(memory_space=pltpu.SMEM),
            pl.BlockSpec((tq, wide), lambda h, i, pt: (i, 4 * sec + h)),
            pl.BlockSpec((t_pad, wide), lambda h, i, pt: (0, h), pipeline_mode=once),
            pl.BlockSpec((t_pad, wide), lambda h, i, pt: (0, sec + h), pipeline_mode=once),
            pl.BlockSpec((1, HEAD_DIM), const2),
            pl.BlockSpec((n_batch, n_heads, HEAD_DIM), lambda h, i, pt: (0, 0, 0), pipeline_mode=once),
            pl.BlockSpec((1, R), const2),
            pl.BlockSpec((R, R), const2, pipeline_mode=once),
            pl.BlockSpec((R, R), const2, pipeline_mode=once),
            pl.BlockSpec(memory_space=pl.ANY),
            pl.BlockSpec(memory_space=pl.ANY),
        ],
        out_specs=[
            pl.BlockSpec((tq, wide), lambda h, i, pt: (i, h)),
            pl.BlockSpec((n_batch, n_heads, HEAD_DIM), lambda h, i, pt: (0, 0, 0)),
        ],
        scratch_shapes=[
            pltpu.VMEM((ATT_HEADS, tq, HEAD_DIM), f32),
            pltpu.VMEM((ATT_HEADS, tq, LANES), f32),
            pltpu.VMEM((DEC_SLOTS, R, HEAD_DIM), f32),
            pltpu.SemaphoreType.DMA((DEC_SLOTS,)),
            pltpu.VMEM((n_pages, R), f32),
            pltpu.VMEM((n_pages, R), f32),
            pltpu.VMEM((n_heads, HEAD_DIM), f32),
            pltpu.SMEM((1,), jnp.int32),
        ],
    )
    return pl.pallas_call(
        functools.partial(_sb_attn_kernel, scale=HEAD_DIM ** -0.5, n_heads=n_heads, n_pages=n_pages,
                          n_batch=n_batch, n_units=n_units),
        grid_spec=grid_spec,
        out_shape=[jax.ShapeDtypeStruct((t_pad, hd), f32),
                   jax.ShapeDtypeStruct((n_batch, n_heads, HEAD_DIM), f32)],
        compiler_params=pltpu.CompilerParams(
            dimension_semantics=("arbitrary", "arbitrary"), vmem_limit_bytes=VMEM_LIMIT),
    )(page_table, sb_bias, p, kvb, kvb, sb_norm, q_dec, dec_bias_lanes, msuf, mtot, cache_k, cache_v)


def _decode_stream(pt_ref, q_ref, bias_ref, msuf_ref, mtot_ref, ck_ref, cv_ref, o_ref,
                   buf, sem, z_scr, a_scr, dacc, *, n_heads, n_pages, n_batch, n_units, scale):
    H = n_heads
    R = buf.shape[1]
    U = DEC_UNROLL
    per_unit = 2 * U
    n_groups = n_pages // U
    ns = DEC_SLOTS
    ahead = ns // per_unit
    assert ns % per_unit == 0 and n_units >= (n_batch + 1) * n_groups and n_units >= ahead

    def unit_coords(u):
        s = u // n_groups
        g = u - s * n_groups
        return s, g, jnp.minimum(s, n_batch - 1), jnp.clip(s - 1, 0, n_batch - 1)

    def slot_of(u, t):
        return (u * per_unit) % ns + t

    def start_unit(u):
        _, g, b_key, b_val = unit_coords(u)
        for t in range(U):
            pltpu.make_async_copy(ck_ref.at[pt_ref[b_key, g * U + t]], buf.at[slot_of(u, t)],
                                  sem.at[slot_of(u, t)]).start()
            pltpu.make_async_copy(cv_ref.at[pt_ref[b_val, g * U + t]], buf.at[slot_of(u, U + t)],
                                  sem.at[slot_of(u, U + t)]).start()

    def wait(slot):
        pltpu.make_async_copy(ck_ref.at[0], buf.at[slot], sem.at[slot]).wait()

    def prologue():
        a_scr[...] = jnp.zeros(a_scr.shape, f32)
        dacc[...] = jnp.zeros(dacc.shape, f32)
        for u in range(ahead):
            start_unit(u)

    def own_lanes():
        return (_iota2((H, R), 1) % H) == _iota2((H, R), 0)

    def wait_unit(u):
        for t in range(per_unit):
            wait(slot_of(u, t))

    def key_page(u, t):
        _, _, b_key, _ = unit_coords(u)
        qb = (q_ref[b_key] * (scale * LOG2E)).astype(bf16)
        zz = _dot_nt(qb, buf[slot_of(u, t)].astype(bf16))
        return jnp.sum(jnp.where(own_lanes(), zz, 0.0), axis=0, keepdims=True)

    def key_store(u, rows):
        _, g, _, _ = unit_coords(u)
        z_scr[pl.ds(pl.multiple_of(g * U, U), U), :] = jnp.concatenate(rows, axis=0)

    def value_page(u, t):
        _, g, _, _ = unit_coords(u)
        a_row = a_scr[pl.ds(pl.multiple_of(g * U, U), U), :][t:t + 1, :]
        abd = jnp.where(own_lanes(), jnp.broadcast_to(a_row, (H, R)), 0.0).astype(bf16)
        return _dot(abd, buf[slot_of(u, U + t)].astype(bf16))

    def form_weights():
        z2 = z_scr[...] + bias_ref[...] * LOG2E
        nl = _softplus2(z2)
        nlb = nl.astype(bf16)
        s_incl = _dot(nlb, msuf_ref[...])
        tot = _dot(nlb, mtot_ref[...])
        later = jnp.where(_iota2((n_pages, n_pages), 1) > _iota2((n_pages, n_pages), 0), 1.0, 0.0).astype(bf16)
        carry = _dot3_right(later, tot)
        a_scr[...] = jnp.exp2(z2 - s_incl - carry)
        dacc[...] = jnp.zeros(dacc.shape, f32)

    def finish(u, value_sum):
        s, g, _, _ = unit_coords(u)

        @pl.when(u + ahead < n_units)
        def _():
            start_unit(u + ahead)

        dacc[...] += value_sum
        last_group = g == n_groups - 1

        @pl.when(jnp.logical_and(last_group, jnp.logical_and(s >= 1, s <= n_batch)))
        def _():
            o_ref[s - 1] = dacc[...]

        @pl.when(jnp.logical_and(last_group, s < n_batch))
        def _():
            form_weights()

    return types.SimpleNamespace(prologue=prologue, wait=wait_unit, key_page=key_page, key_store=key_store,
                                 value_page=value_page, finish=finish)


def _dec_finish_kernel(o_ref, q_ref, k_ref, v_ref, bias_ref, gn_ref, out_ref, *, past, dec_seq, scale):
    kpos = past + _iota2(o_ref.shape, 1) * 0 + (dec_seq - 1)
    qpos = past + _iota2(o_ref.shape, 1) * 0 + (dec_seq - 1)
    valid = kpos < qpos
    z = jnp.sum(q_ref[...] * k_ref[...], axis=-1, keepdims=True) * scale + bias_ref[...]
    beta = _sigmoid(z)
    o = o_ref[...]
    o = jnp.where(valid, o * (1.0 - beta) + beta * v_ref[...], o)
    out_ref[...] = o * lax.rsqrt(jnp.mean(o * o, axis=-1, keepdims=True) + EPS) * gn_ref[...]


def _dec_finish(o_past, q_s, k_s, v_s, bias_col, sb_norm, past, dec_seq):
    n = o_past.shape[0]
    full = pl.BlockSpec((n, HEAD_DIM), lambda i: (0, 0))
    return pl.pallas_call(
        functools.partial(_dec_finish_kernel, past=past, dec_seq=dec_seq, scale=HEAD_DIM ** -0.5),
        grid=(1,),
        in_specs=[full, full, full, full, full, pl.BlockSpec((1, HEAD_DIM), lambda i: (0, 0))],
        out_specs=full,
        out_shape=jax.ShapeDtypeStruct((n, HEAD_DIM), f32),
    )(o_past, q_s, k_s, v_s, bias_col, sb_norm)


def _gdn_step_kernel(x_ref, cs_ref, cw_ref, blog_ref, alog_ref, adec_ref, dtb_ref, gn_ref, s_ref,
                     o_ref, s_out_ref, *, n_heads):
    H = n_heads
    x = x_ref[0]
    cs = cs_ref[0]
    y = x[0:3 * H] * cw_ref[CONV_W - 1]
    for i in range(CONV_W - 1):
        y = y + cs[i] * cw_ref[i]
    y = y * _sigmoid(y)
    q = y[0:H]
    k = y[H:2 * H]
    v = y[2 * H:3 * H]
    z = x[3 * H:4 * H]
    q = q * lax.rsqrt(jnp.sum(q * q, axis=-1, keepdims=True) + EPS) * (HEAD_DIM ** -0.5)
    k = k * lax.rsqrt(jnp.sum(k * k, axis=-1, keepdims=True) + EPS)
    beta = _sigmoid(blog_ref[0])
    g = -jnp.exp(alog_ref[...]) * _softplus(adec_ref[0] + dtb_ref[...])
    eg = jnp.exp(g)

    eye = jnp.where(_iota2((HEAD_DIM, HEAD_DIM), 0) == _iota2((HEAD_DIM, HEAD_DIM), 1), 1.0, 0.0).astype(bf16)
    k3 = _split3(k)
    q3 = _split3(q)
    kcol = _dot_nt(eye, k3[0]) + _dot_nt(eye, k3[1]) + _dot_nt(eye, k3[2])
    qcol = _dot_nt(eye, q3[0]) + _dot_nt(eye, q3[1]) + _dot_nt(eye, q3[2])

    outs = []
    for h in range(H):
        s = s_ref[0, h]
        kc = kcol[:, h:h + 1]
        egh = eg[h:h + 1, :]
        ks = jnp.sum(kc * s, axis=0, keepdims=True)
        vn = (v[h:h + 1, :] - egh * ks) * beta[h:h + 1, :]
        s_new = s * egh + kc * vn
        s_out_ref[0, h] = s_new
        outs.append(jnp.sum(qcol[:, h:h + 1] * s_new, axis=0, keepdims=True))
    o = jnp.concatenate(outs, axis=0)
    o = o * lax.rsqrt(jnp.mean(o * o, axis=-1, keepdims=True) + EPS) * gn_ref[...]
    o_ref[0] = o * (z * _sigmoid(z))


def _gdn_step(x_s, conv_state, conv_w, blog, adec, alog, dtb, gdn_norm, state, n_heads):
    n_batch = x_s.shape[0]
    H = n_heads
    const2 = lambda b: (0, 0)
    return pl.pallas_call(
        functools.partial(_gdn_step_kernel, n_heads=n_heads),
        grid=(n_batch,),
        in_specs=[
            pl.BlockSpec((1, 4 * H, HEAD_DIM), lambda b: (b, 0, 0)),
            pl.BlockSpec((1, CONV_W - 1, 3 * H, HEAD_DIM), lambda b: (b, 0, 0, 0)),
            pl.BlockSpec((CONV_W, 3 * H, HEAD_DIM), lambda b: (0, 0, 0)),
            pl.BlockSpec((1, H, LANES), lambda b: (b, 0, 0)),
            pl.BlockSpec((H, LANES), const2),
            pl.BlockSpec((1, H, LANES), lambda b: (b, 0, 0)),
            pl.BlockSpec((H, LANES), const2),
            pl.BlockSpec((1, HEAD_DIM), const2),
            pl.BlockSpec((1, H, HEAD_DIM, HEAD_DIM), lambda b: (b, 0, 0, 0)),
        ],
        out_specs=[
            pl.BlockSpec((1, H, HEAD_DIM), lambda b: (b, 0, 0)),
            pl.BlockSpec((1, H, HEAD_DIM, HEAD_DIM), lambda b: (b, 0, 0, 0)),
        ],
        out_shape=[
            jax.ShapeDtypeStruct((n_batch, H, HEAD_DIM), f32),
            jax.ShapeDtypeStruct((n_batch, H, HEAD_DIM, HEAD_DIM), f32),
        ],
        compiler_params=pltpu.CompilerParams(dimension_semantics=("parallel",)),
    )(x_s, conv_state, conv_w, blog, alog, adec, dtb, gdn_norm, state)


def _place_rows_kernel(a_s_ref, b_s_ref, a_in_ref, b_in_ref, a_ref, b_ref):
    del a_in_ref, b_in_ref
    a_ref[...] = a_s_ref[...]
    b_ref[...] = b_s_ref[...]


def _place_rows(a, b, a_rows, b_rows, row0):
    n, width = a_rows.shape
    rb = SUBLANES
    assert row0 % rb == 0 and n % rb == 0
    small = pl.BlockSpec((rb, width), lambda i: (i, 0))
    big = pl.BlockSpec((rb, width), lambda i: (row0 // rb + i, 0))
    return pl.pallas_call(
        _place_rows_kernel,
        grid=(n // rb,),
        in_specs=[small, small, pl.BlockSpec(memory_space=pl.ANY), pl.BlockSpec(memory_space=pl.ANY)],
        out_specs=[big, big],
        out_shape=[jax.ShapeDtypeStruct(a.shape, a.dtype), jax.ShapeDtypeStruct(b.shape, b.dtype)],
        input_output_aliases={2: 0, 3: 1},
    )(a_rows, b_rows, a, b)


def _outproj_kernel(x_ref, oa_ref, ob_ref, wa_ref, wb_ref, g_ref, x1_ref, hn_ref):
    x1 = x_ref[...] + _dot(oa_ref[...].astype(bf16), wa_ref[...]) + _dot(ob_ref[...].astype(bf16), wb_ref[...])
    x1_ref[...] = x1
    hn = x1 * lax.rsqrt(jnp.mean(x1 * x1, axis=-1, keepdims=True) + EPS) * g_ref[...]
    hn_ref[...] = hn.astype(bf16)


def _outproj(x_all, oa, ob, w_a, w_b, norm_ffn):
    t_pad, d = x_all.shape
    hd = oa.shape[1]
    tm = TM_OUT
    return pl.pallas_call(
        _outproj_kernel,
        grid=(t_pad // tm,),
        in_specs=[
            pl.BlockSpec((tm, d), lambda i: (i, 0)),
            pl.BlockSpec((tm, hd), lambda i: (i, 0)),
            pl.BlockSpec((tm, hd), lambda i: (i, 0)),
            pl.BlockSpec((hd, d), lambda i: (0, 0)),
            pl.BlockSpec((hd, d), lambda i: (0, 0)),
            pl.BlockSpec((1, d), lambda i: (0, 0)),
        ],
        out_specs=[pl.BlockSpec((tm, d), lambda i: (i, 0)), pl.BlockSpec((tm, d), lambda i: (i, 0))],
        out_shape=[jax.ShapeDtypeStruct((t_pad, d), f32), jax.ShapeDtypeStruct((t_pad, d), bf16)],
        compiler_params=pltpu.CompilerParams(
            dimension_semantics=("parallel",), vmem_limit_bytes=VMEM_LIMIT),
    )(x_all, oa, ob, w_a, w_b, norm_ffn)


def _ffn_kernel(x1_ref, hn_ref, wg_ref, wu_ref, wd_ref, y_ref):
    f = pl.program_id(1)

    @pl.when(f == 0)
    def _():
        y_ref[...] = x1_ref[...]

    hn = hn_ref[...]
    gate = _dot(hn, wg_ref[...])
    up = _dot(hn, wu_ref[...])
    act = (gate * _sigmoid(gate) * up).astype(bf16)
    y_ref[...] += _dot(act, wd_ref[...])


def _ffn(x1, hn, w_gate, w_up, w_down):
    t_pad, d = x1.shape
    d_ff = w_gate.shape[1]
    tm, tf = TM_PROJ, TF_FFN
    assert d_ff % tf == 0
    return pl.pallas_call(
        _ffn_kernel,
        grid=(t_pad // tm, d_ff // tf),
        in_specs=[
            pl.BlockSpec((tm, d), lambda i, f: (i, 0)),
            pl.BlockSpec((tm, d), lambda i, f: (i, 0)),
            pl.BlockSpec((d, tf), lambda i, f: (0, f)),
            pl.BlockSpec((d, tf), lambda i, f: (0, f)),
            pl.BlockSpec((tf, d), lambda i, f: (f, 0)),
        ],
        out_specs=pl.BlockSpec((tm, d), lambda i, f: (i, 0)),
        out_shape=jax.ShapeDtypeStruct((t_pad, d), f32),
        compiler_params=pltpu.CompilerParams(
            dimension_semantics=("parallel", "arbitrary"), vmem_limit_bytes=VMEM_LIMIT),
    )(x1, hn, w_gate, w_up, w_down)


def _lane_pad(v, offset):
    return jnp.zeros((1, LANES), f32).at[0, offset:offset + v.shape[0]].set(v)


def kernel(x_prompt, x_sample, cache_k, cache_v, state_gdn, state_conv, page_table, meta, norm_mix, w_in,
           conv_w, a_log, dt_bias, gdn_norm, q_norm, k_norm, sb_norm, sb_bias, w_out, norm_ffn, w_gate,
           w_up, w_down):
    depth = w_in.shape[0]
    bsz, seq, d = x_prompt.shape
    dbsz, dec_seq, _ = x_sample.shape
    assert depth == 1 and bsz == 1 and dec_seq == 1
    H = d // (2 * HEAD_DIM)
    hd = H * HEAD_DIM
    n_pages, page_size = page_table.shape[1], cache_k.shape[2]
    past = n_pages * page_size
    t_valid = N_META + seq
    t_used = t_valid + dbsz
    t_pad = -(-t_used // ROW_ALIGN) * ROW_ALIGN
    l = 0

    x_all = jnp.concatenate(
        [meta.astype(f32), x_prompt[0], x_sample[:, 0], jnp.zeros((t_pad - t_used, d), f32)], axis=0)

    w = w_in[l]
    o_z = 3 * hd
    o_b = o_z + hd
    o_q = o_b + 2 * H
    w16 = w.astype(bf16)
    w_sec = jnp.concatenate([w16[:, :o_b], w16[:, o_q:]], axis=1)
    w_ba = jnp.concatenate([w16[:, o_b:o_q], jnp.zeros((d, LANES - 2 * H), bf16)], axis=1)

    p, ba, bat, kvb = _inproj(x_all, norm_mix[l][None], w_sec, w_ba, w_ba.T, q_norm[l][None], k_norm[l][None], H)

    cw = conv_w[l]
    alog_r = _lane_pad(a_log[l], H)
    dt_r = _lane_pad(dt_bias[l], H)
    oa, s_fin = _gdn_prompt(p, ba, bat, cw, alog_r, dt_r,
                            jnp.broadcast_to(alog_r.T, (LANES, LANES)), jnp.broadcast_to(dt_r.T, (LANES, LANES)),
                            gdn_norm[l][None], H, t_valid)

    ps_rows = p[t_valid:t_used]
    ba_s = ba[t_valid:t_used]
    x_s = ps_rows[:, :4 * hd].reshape(dbsz, 4 * H, HEAD_DIM)
    blog = jnp.broadcast_to(ba_s[:, :H, None], (dbsz, H, LANES))
    adec = jnp.broadcast_to(ba_s[:, H:2 * H, None], (dbsz, H, LANES))
    oa_s, s_new = _gdn_step(
        x_s, state_conv[l].reshape(dbsz, CONV_W - 1, 3 * H, HEAD_DIM), cw.reshape(CONV_W, 3 * H, HEAD_DIM),
        blog, adec, jnp.broadcast_to(a_log[l][:, None], (H, LANES)),
        jnp.broadcast_to(dt_bias[l][:, None], (H, LANES)), gdn_norm[l][None], state_gdn[l], H)

    q_s = ps_rows[:, 4 * hd:5 * hd]
    k_s = ps_rows[:, 5 * hd:6 * hd]
    v_s = ps_rows[:, 6 * hd:7 * hd]
    R = page_size * H
    ridx = jnp.arange(R)
    same_head = (ridx[:, None] % H) == (ridx[None, :] % H)
    msuf = jnp.logical_and(same_head, (ridx[:, None] // H) >= (ridx[None, :] // H)).astype(bf16)
    mtot = same_head.astype(bf16)
    bias_lanes = jnp.tile(sb_bias[l], page_size)[None]
    ob, o_past = _sb_attn_prompt_and_decode(
        p, kvb, sb_bias[l], sb_norm[l][None], page_table, q_s.reshape(dbsz, H, HEAD_DIM), bias_lanes, msuf, mtot,
        cache_k[l].reshape(-1, R, HEAD_DIM), cache_v[l].reshape(-1, R, HEAD_DIM), H)
    bias_col = jnp.broadcast_to(jnp.tile(sb_bias[l], dbsz)[:, None], (dbsz * H, HEAD_DIM))
    ob_s = _dec_finish(o_past.reshape(dbsz * H, HEAD_DIM), q_s.reshape(dbsz * H, HEAD_DIM),
                       k_s.reshape(dbsz * H, HEAD_DIM), v_s.reshape(dbsz * H, HEAD_DIM),
                       bias_col, sb_norm[l][None], past, dec_seq)

    oa, ob = _place_rows(oa, ob, oa_s.reshape(dbsz, hd), ob_s.reshape(dbsz, hd), t_valid)

    wo = w_out[l].astype(bf16)
    x1, hn = _outproj(x_all, oa, ob, wo[:hd], wo[hd:], norm_ffn[l][None])
    y = _ffn(x1, hn, w_gate[l].astype(bf16), w_up[l].astype(bf16), w_down[l].astype(bf16))

    new_conv_s = jnp.concatenate([state_conv[l][:, 1:], ps_rows[:, None, :3 * hd]], axis=1)
    return (
        y[N_META:t_valid][None],
        y[t_valid:t_used][:, None],
        p[:t_valid, 5 * hd:6 * hd].reshape(1, 1, t_valid, H, HEAD_DIM),
        p[:t_valid, 6 * hd:7 * hd].reshape(1, 1, t_valid, H, HEAD_DIM),
        s_fin[None, None],
        p[t_valid - (CONV_W - 1):t_valid, :3 * hd][None, None],
        k_s.reshape(1, dbsz, 1, H, HEAD_DIM),
        v_s.reshape(1, dbsz, 1, H, HEAD_DIM),
        s_new[None],
        new_conv_s[None],
    )
```

```python
import functools

import jax
import jax.numpy as jnp
from jax import lax
from jax.experimental import pallas as pl
from jax.experimental.pallas import tpu as pltpu

EPS = 1e-6
N_META = 16
HEAD_DIM = 128
CONV_W = 4
LOG2E = 1.4426950408889634
LN2 = 0.6931471805599453

SUBLANES = 8
LANES = 128
VMEM_LIMIT = 56 * 1024 * 1024

ROW_ALIGN = 768
TM_PROJ = 768
TM_OUT = 384
TF_FFN = 512
GDN_CHUNK = 128
TQ_ATT = 768
TK_ATT = 768
ATT_GROUP = 256
ATT_HEADS = 2
DEC_SLOTS = 32
DEC_UNROLL = 8

f32 = jnp.float32
bf16 = jnp.bfloat16


def _dot(a, b):
    return jnp.dot(a, b, preferred_element_type=f32)


def _dot_nt(a, b):
    return lax.dot_general(a, b, (((1,), (1,)), ((), ())), preferred_element_type=f32)


def _dot_tn(a, b):
    return lax.dot_general(a, b, (((0,), (0,)), ((), ())), preferred_element_type=f32)


def _split3(x):
    hi = x.astype(bf16)
    r = x - hi.astype(f32)
    mid = r.astype(bf16)
    lo = (r - mid.astype(f32)).astype(bf16)
    return hi, mid, lo


def _dot3_left(x, m_bf16):
    hi, mid, lo = _split3(x)
    return _dot(hi, m_bf16) + _dot(mid, m_bf16) + _dot(lo, m_bf16)


def _dot3_right(m_bf16, x):
    hi, mid, lo = _split3(x)
    return _dot(m_bf16, hi) + _dot(m_bf16, mid) + _dot(m_bf16, lo)


def _sigmoid(x):
    return 1.0 / (1.0 + jnp.exp(-x))


def _softplus(x):
    return jnp.maximum(x, 0.0) + jnp.log(1.0 + jnp.exp(-jnp.abs(x)))


def _softplus2(z2):
    return jnp.maximum(z2, 0.0) + jnp.log2(1.0 + jnp.exp2(-jnp.abs(z2)))


INV_BASE = 16


def _unit_lower_inverse_minus_eye(mats, ii, jj):
    C = mats[0].shape[0]
    idx = range(len(mats))
    diag = (ii // INV_BASE) == (jj // INV_BASE)
    d = [jnp.where(diag, a, 0.0) for a in mats]
    n = [-x for x in d]
    p = d
    for _ in range(INV_BASE.bit_length() - 2):
        pb = [x.astype(bf16) for x in p]
        p = [_dot(x, x) for x in pb]
        pb = [x.astype(bf16) for x in p]
        n = [n[i] + p[i] + _dot(n[i].astype(bf16), pb[i]) for i in idx]
    bs = INV_BASE
    while bs < C:
        pair = jnp.logical_and((ii // (2 * bs)) == (jj // (2 * bs)), (ii // bs) != (jj // bs))
        off = [jnp.where(pair, a, 0.0) for a in mats]
        x = [off[i] + _dot(off[i].astype(bf16), n[i].astype(bf16)) for i in idx]
        n = [n[i] - (x[i] + _dot(n[i].astype(bf16), x[i].astype(bf16))) for i in idx]
        bs *= 2
    return n


def _iota2(shape, dim):
    return lax.broadcasted_iota(jnp.int32, shape, dim)


def _inproj_kernel(x_ref, g_ref, w_ref, wba_ref, wbat_ref, qn_ref, kn_ref,
                   p_ref, ba_ref, bat_ref, kvb_ref, kf_ref, vf_ref, xn_scr, *, n_heads):
    j = pl.program_id(1)

    @pl.when(j == 0)
    def _():
        x = x_ref[...]
        xn = x * lax.rsqrt(jnp.mean(x * x, axis=-1, keepdims=True) + EPS) * g_ref[...]
        xnb = xn.astype(bf16)
        xn_scr[...] = xnb
        ba_ref[...] = _dot(xnb, wba_ref[...])
        bat_ref[...] = _dot_nt(wbat_ref[...], xnb)

    y = _dot(xn_scr[...], w_ref[...])

    def headnorm(gain_ref, copy_ref, rows_ref):
        for h in range(n_heads):
            sl = slice(h * HEAD_DIM, (h + 1) * HEAD_DIM)
            yh = y[:, sl]
            ms = jnp.mean(yh * yh, axis=-1, keepdims=True)
            yn = yh * lax.rsqrt(ms + EPS) * gain_ref[...]
            p_ref[:, sl] = yn
            if copy_ref is not None:
                copy_ref[:, sl] = yn.astype(bf16)
                rows_ref[:, sl] = yn

    @pl.when(j == 4)
    def _():
        headnorm(qn_ref, None, None)

    @pl.when(j == 5)
    def _():
        headnorm(kn_ref, kvb_ref, kf_ref)

    @pl.when(j == 6)
    def _():
        p_ref[...] = y
        kvb_ref[...] = y.astype(bf16)
        vf_ref[...] = y

    @pl.when(j < 4)
    def _():
        p_ref[...] = y


def _inproj(x_all, norm_mix, w_sec, w_ba, w_bat, q_norm, k_norm, n_heads, t_valid):
    t_pad, d = x_all.shape
    hd = n_heads * HEAD_DIM
    tm = TM_PROJ
    grid = (t_pad // tm, 7)
    return pl.pallas_call(
        functools.partial(_inproj_kernel, n_heads=n_heads),
        grid=grid,
        in_specs=[
            pl.BlockSpec((tm, d), lambda i, j: (i, 0)),
            pl.BlockSpec((1, d), lambda i, j: (0, 0)),
            pl.BlockSpec((d, hd), lambda i, j: (0, j)),
            pl.BlockSpec((d, LANES), lambda i, j: (0, 0)),
            pl.BlockSpec((LANES, d), lambda i, j: (0, 0)),
            pl.BlockSpec((1, HEAD_DIM), lambda i, j: (0, 0)),
            pl.BlockSpec((1, HEAD_DIM), lambda i, j: (0, 0)),
        ],
        out_specs=[
            pl.BlockSpec((tm, hd), lambda i, j: (i, j)),
            pl.BlockSpec((tm, LANES), lambda i, j: (i, 0)),
            pl.BlockSpec((LANES, tm), lambda i, j: (0, i)),
            pl.BlockSpec((tm, hd), lambda i, j: (i, jnp.clip(j - 5, 0, 1))),
            pl.BlockSpec((tm, hd), lambda i, j: (i, 0)),
            pl.BlockSpec((tm, hd), lambda i, j: (i, 0)),
        ],
        out_shape=[
            jax.ShapeDtypeStruct((t_pad, 7 * hd), f32),
            jax.ShapeDtypeStruct((t_pad, LANES), f32),
            jax.ShapeDtypeStruct((LANES, t_pad), f32),
            jax.ShapeDtypeStruct((t_pad, 2 * hd), bf16),
            jax.ShapeDtypeStruct((t_valid, hd), f32),
            jax.ShapeDtypeStruct((t_valid, hd), f32),
        ],
        scratch_shapes=[pltpu.VMEM((tm, d), bf16)],
        compiler_params=pltpu.CompilerParams(
            dimension_semantics=("parallel", "arbitrary"), vmem_limit_bytes=VMEM_LIMIT),
    )(x_all, norm_mix, w_sec, w_ba, w_bat, q_norm, k_norm)


def _gdn_kernel(x_ref, ba_ref, bat_ref, cw_ref, alog_r_ref, dt_r_ref, alog_c_ref, dt_c_ref, gn_ref,
                oa_ref, s_ref, xbuf, *, n_heads, t_valid):
    c = pl.program_id(0)
    C = GDN_CHUNK
    hd = n_heads * HEAD_DIM
    hist = CONV_W - 1

    @pl.when(c == 0)
    def _():
        xbuf[0:SUBLANES, :] = jnp.zeros((SUBLANES, 3 * hd), f32)
        s_ref[...] = jnp.zeros(s_ref.shape, f32)

    xbuf[SUBLANES:SUBLANES + C, :] = x_ref[:, 0:3 * hd]
    y = xbuf[SUBLANES:SUBLANES + C, :] * cw_ref[hist:hist + 1, :]
    for i in range(hist):
        y = y + xbuf[SUBLANES - hist + i:SUBLANES - hist + i + C, :] * cw_ref[i:i + 1, :]
    y = y * _sigmoid(y)
    xbuf[SUBLANES - hist:SUBLANES, :] = xbuf[SUBLANES + C - hist:SUBLANES + C, :]

    ii = _iota2((C, C), 0)
    jj = _iota2((C, C), 1)
    incl = ii >= jj
    strict = ii > jj
    low_ones = jnp.where(incl, 1.0, 0.0).astype(bf16)
    up_ones = jnp.where(jj >= ii, 1.0, 0.0).astype(bf16)

    row_ok = (c * C + _iota2((C, LANES), 0)) < t_valid
    nr = 2 * n_heads
    lane_ok = (c * C + _iota2((nr, C), 1)) < t_valid
    ba = ba_ref[...]
    bat = bat_ref[0:nr, :]
    beta_cols = jnp.where(row_ok, _sigmoid(ba), 0.0)
    g_cols = jnp.where(row_ok, -jnp.exp(alog_r_ref[...]) * _softplus(ba + dt_r_ref[...]), 0.0)
    g_rows = jnp.where(lane_ok, -jnp.exp(alog_c_ref[0:nr, :]) * _softplus(bat + dt_c_ref[0:nr, :]), 0.0)
    gc_cols = _dot3_right(low_ones, g_cols)
    gc_rows = _dot3_left(g_rows, up_ones)

    heads = range(n_heads)

    def head_cols(off, h):
        return slice(off + h * HEAD_DIM, off + (h + 1) * HEAD_DIM)

    q = [y[:, head_cols(0, h)] for h in heads]
    k = [y[:, head_cols(hd, h)] for h in heads]
    v = [y[:, head_cols(2 * hd, h)] for h in heads]
    q = [x * lax.rsqrt(jnp.sum(x * x, axis=-1, keepdims=True) + EPS) * (HEAD_DIM ** -0.5) for x in q]
    k = [x * lax.rsqrt(jnp.sum(x * x, axis=-1, keepdims=True) + EPS) for x in k]

    beta = [beta_cols[:, h:h + 1] for h in heads]
    gcc = [gc_cols[:, n_heads + h:n_heads + h + 1] for h in heads]
    gcr = [gc_rows[n_heads + h:n_heads + h + 1, :] for h in heads]
    g_last = [gc_cols[C - 1:C, n_heads + h:n_heads + h + 1] for h in heads]
    decay = [jnp.where(incl, jnp.exp(jnp.minimum(gcc[h] - gcr[h], 0.0)), 0.0) for h in heads]
    kb = [k[h] * beta[h] for h in heads]
    kbf = [k[h].astype(bf16) for h in heads]
    a = [jnp.where(strict, _dot_nt(kb[h].astype(bf16), kbf[h]) * decay[h], 0.0) for h in heads]
    qk = [(_dot_nt(q[h].astype(bf16), kbf[h]) * decay[h]).astype(bf16) for h in heads]

    n = _unit_lower_inverse_minus_eye(a, ii, jj)
    egc = [jnp.exp(gcc[h]) for h in heads]
    rhs_u = [v[h] * beta[h] for h in heads]
    rhs_w = [kb[h] * egc[h] for h in heads]
    nb = [n[h].astype(bf16) for h in heads]
    u = [rhs_u[h] + _dot(nb[h], rhs_u[h].astype(bf16)) for h in heads]
    w = [(rhs_w[h] + _dot(nb[h], rhs_w[h].astype(bf16))).astype(bf16) for h in heads]
    q_dec = [(q[h] * egc[h]).astype(bf16) for h in heads]
    k_dec = [(k[h] * jnp.exp(g_last[h] - gcc[h])).astype(bf16) for h in heads]

    s = [s_ref[h] for h in heads]
    sb = [s[h].astype(bf16) for h in heads]
    vnb = [(u[h] - _dot(w[h], sb[h])).astype(bf16) for h in heads]
    o = [_dot(q_dec[h], sb[h]) + _dot(qk[h], vnb[h]) for h in heads]
    for h in heads:
        s_ref[h] = s[h] * jnp.exp(g_last[h]) + _dot_tn(k_dec[h], vnb[h])
    for h in heads:
        z = x_ref[:, head_cols(3 * hd, h)]
        on = o[h] * lax.rsqrt(jnp.mean(o[h] * o[h], axis=-1, keepdims=True) + EPS) * gn_ref[...]
        oa_ref[:, head_cols(0, h)] = on * (z * _sigmoid(z))


def _gdn_prompt(p, ba, bat, conv_w, alog_r, dt_r, alog_c, dt_c, gdn_norm, n_heads, t_valid):
    t_pad = p.shape[0]
    hd = n_heads * HEAD_DIM
    C = GDN_CHUNK
    const = lambda c: (0, 0)
    return pl.pallas_call(
        functools.partial(_gdn_kernel, n_heads=n_heads, t_valid=t_valid),
        grid=(t_pad // C,),
        in_specs=[
            pl.BlockSpec((C, 4 * hd), lambda c: (c, 0)),
            pl.BlockSpec((C, LANES), lambda c: (c, 0)),
            pl.BlockSpec((LANES, C), lambda c: (0, c)),
            pl.BlockSpec((CONV_W, 3 * hd), const),
            pl.BlockSpec((1, LANES), const),
            pl.BlockSpec((1, LANES), const),
            pl.BlockSpec((LANES, LANES), const),
            pl.BlockSpec((LANES, LANES), const),
            pl.BlockSpec((1, HEAD_DIM), const),
        ],
        out_specs=[
            pl.BlockSpec((C, hd), lambda c: (c, 0)),
            pl.BlockSpec((n_heads, HEAD_DIM, HEAD_DIM), lambda c: (0, 0, 0)),
        ],
        out_shape=[
            jax.ShapeDtypeStruct((t_pad, hd), f32),
            jax.ShapeDtypeStruct((n_heads, HEAD_DIM, HEAD_DIM), f32),
        ],
        scratch_shapes=[pltpu.VMEM((SUBLANES + C, 3 * hd), f32)],
        compiler_params=pltpu.CompilerParams(
            dimension_semantics=("arbitrary",), vmem_limit_bytes=VMEM_LIMIT),
    )(p, ba, bat, conv_w, alog_r, dt_r, alog_c, dt_c, gdn_norm)


def _sb_attn_kernel(pt_ref, bias_ref, q_ref, k_ref, v_ref, gn_ref,
                    qd_ref, dbias_ref, msuf_ref, mtot_ref, ck_ref, cv_ref,
                    o_ref, od_ref,
                    acc_scr, carry_scr, buf, sem, z_scr, a_scr, dacc, cnt,
                    *, scale, n_heads, n_pages, n_batch, units_per_span):
    hp = pl.program_id(0)
    i = pl.program_id(1)
    first_step = jnp.logical_and(hp == 0, i == 0)
    last_step = jnp.logical_and(hp == pl.num_programs(0) - 1, i == pl.num_programs(1) - 1)
    dec_prologue, dec_unit, n_units = _decode_units(
        pt_ref, qd_ref, dbias_ref, msuf_ref, mtot_ref, ck_ref, cv_ref, od_ref, buf, sem, z_scr, a_scr, dacc,
        n_heads=n_heads, n_pages=n_pages, n_batch=n_batch, scale=scale)

    @pl.when(first_step)
    def _():
        cnt[0] = 0
        dec_prologue()

    def run_units(n):
        def one(_, c):
            u = cnt[0]

            @pl.when(u < n_units)
            def _():
                dec_unit(u)
                cnt[0] = u + 1

            return c

        lax.fori_loop(0, n, one, 0)

    tq, tk, G = TQ_ATT, TK_ATT, ATT_GROUP
    heads = range(ATT_HEADS)
    groups = range(tk // G)

    def hcols(hh):
        return slice(hh * HEAD_DIM, (hh + 1) * HEAD_DIM)

    lane = _iota2((tq, HEAD_DIM), 1)
    ones3 = jnp.where(lane < 3, 1.0, 0.0).astype(bf16)
    q2, k_tail = [], []
    for hh in heads:
        q2.append(jnp.concatenate([(q_ref[:, hcols(hh)] * (scale * LOG2E)).astype(bf16), ones3], axis=1))
        b_hi, b_mid, b_lo = _split3(jnp.full((tk, HEAD_DIM), bias_ref[hp * ATT_HEADS + hh] * LOG2E, f32))
        klane = _iota2((tk, HEAD_DIM), 1)
        tail = jnp.where(klane == 0, b_hi.astype(f32),
                         jnp.where(klane == 1, b_mid.astype(f32), jnp.where(klane == 2, b_lo.astype(f32), 0.0)))
        k_tail.append(tail.astype(bf16))

    mo = jnp.where(_iota2((G, G), 0) >= _iota2((G, G), 1), 1.0, 0.0).astype(bf16)

    acc_scr[...] = jnp.zeros(acc_scr.shape, f32)
    carry_scr[...] = jnp.zeros(carry_scr.shape, f32)

    def span(start, masked):
        kaug = [jnp.concatenate([k_ref[pl.ds(start, tk), hcols(hh)], k_tail[hh]], axis=1) for hh in heads]
        z2 = [_dot_nt(q2[hh], kaug[hh]) for hh in heads]
        nl = [_softplus2(z) for z in z2]
        if masked:
            valid = _iota2((tq, tk), 1) < _iota2((tq, tk), 0)
            nl = [jnp.where(valid, x, 0.0) for x in nl]
        nlb = [x.astype(bf16) for x in nl]
        carry = [carry_scr[hh, :, 0:1] for hh in heads]
        parts = [[None] * len(groups) for _ in heads]
        for g in reversed(groups):
            gs = slice(g * G, (g + 1) * G)
            for hh in heads:
                sg = _dot(nlb[hh][:, gs], mo)
                w = jnp.exp2(z2[hh][:, gs] - sg - carry[hh])
                if masked:
                    w = jnp.where(valid[:, gs], w, 0.0)
                parts[hh][g] = w.astype(bf16)
                carry[hh] = carry[hh] + sg[:, 0:1]
        for hh in heads:
            carry_scr[hh] = jnp.broadcast_to(carry[hh], (tq, LANES))
            acc_scr[hh] += _dot(jnp.concatenate(parts[hh], axis=-1), v_ref[pl.ds(start, tk), hcols(hh)])

    span(pl.multiple_of(i * tq, tq), True)
    run_units(units_per_span)

    def body(t, _):
        span(pl.multiple_of((i - 1 - t) * tk, tk), False)
        run_units(units_per_span)
        return 0

    lax.fori_loop(0, i * (tq // tk), body, 0)

    for hh in heads:
        o = acc_scr[hh]
        o_ref[:, hcols(hh)] = o * lax.rsqrt(jnp.mean(o * o, axis=-1, keepdims=True) + EPS) * gn_ref[...]

    @pl.when(last_step)
    def _():
        def rest(_, c):
            u = cnt[0]
            dec_unit(u)
            cnt[0] = u + 1
            return c

        lax.fori_loop(0, n_units - cnt[0], rest, 0)


def _sb_attn_prompt_and_decode(p, kvb, sb_bias, sb_norm, page_table, q_dec, dec_bias_lanes, msuf, mtot,
                               cache_k, cache_v, n_heads):
    t_pad = p.shape[0]
    hd = n_heads * HEAD_DIM
    tq = TQ_ATT
    wide = ATT_HEADS * HEAD_DIM
    n_batch, n_pages = page_table.shape
    R = cache_k.shape[1]
    assert TQ_ATT == TK_ATT and TK_ATT % ATT_GROUP == 0 and n_heads % ATT_HEADS == 0
    assert n_heads == SUBLANES and n_pages % DEC_UNROLL == 0 and DEC_SLOTS % DEC_UNROLL == 0
    assert (2 * n_pages) % DEC_SLOTS == 0
    sec = hd // wide
    nq = t_pad // tq
    n_spans = (n_heads // ATT_HEADS) * (nq * (nq + 1) // 2)
    n_units = n_batch * (2 * (n_pages // DEC_UNROLL) + 1)
    units_per_span = -(-n_units // n_spans)
    const2 = lambda h, i, pt: (0, 0)
    once = pl.Buffered(1)
    grid_spec = pltpu.PrefetchScalarGridSpec(
        num_scalar_prefetch=1,
        grid=(n_heads // ATT_HEADS, nq),
        in_specs=[
            pl.BlockSpec(memory_space=pltpu.SMEM),
            pl.BlockSpec((tq, wide), lambda h, i, pt: (i, 4 * sec + h)),
            pl.BlockSpec((t_pad, wide), lambda h, i, pt: (0, h), pipeline_mode=once),
            pl.BlockSpec((t_pad, wide), lambda h, i, pt: (0, sec + h), pipeline_mode=once),
            pl.BlockSpec((1, HEAD_DIM), const2),
            pl.BlockSpec((n_batch, n_heads, HEAD_DIM), lambda h, i, pt: (0, 0, 0), pipeline_mode=once),
            pl.BlockSpec((1, R), const2),
            pl.BlockSpec((R, R), const2, pipeline_mode=once),
            pl.BlockSpec((R, R), const2, pipeline_mode=once),
            pl.BlockSpec(memory_space=pl.ANY),
            pl.BlockSpec(memory_space=pl.ANY),
        ],
        out_specs=[
            pl.BlockSpec((tq, wide), lambda h, i, pt: (i, h)),
            pl.BlockSpec((n_batch, n_heads, HEAD_DIM), lambda h, i, pt: (0, 0, 0)),
        ],
        scratch_shapes=[
            pltpu.VMEM((ATT_HEADS, tq, HEAD_DIM), f32),
            pltpu.VMEM((ATT_HEADS, tq, LANES), f32),
            pltpu.VMEM((DEC_SLOTS, R, HEAD_DIM), f32),
            pltpu.SemaphoreType.DMA((DEC_SLOTS,)),
            pltpu.VMEM((n_pages, R), f32),
            pltpu.VMEM((n_pages, R), f32),
            pltpu.VMEM((n_heads, HEAD_DIM), f32),
            pltpu.SMEM((1,), jnp.int32),
        ],
    )
    return pl.pallas_call(
        functools.partial(_sb_attn_kernel, scale=HEAD_DIM ** -0.5, n_heads=n_heads, n_pages=n_pages,
                          n_batch=n_batch, units_per_span=units_per_span),
        grid_spec=grid_spec,
        out_shape=[jax.ShapeDtypeStruct((t_pad, hd), f32),
                   jax.ShapeDtypeStruct((n_batch, n_heads, HEAD_DIM), f32)],
        compiler_params=pltpu.CompilerParams(
            dimension_semantics=("arbitrary", "arbitrary"), vmem_limit_bytes=VMEM_LIMIT),
    )(page_table, sb_bias, p, kvb, kvb, sb_norm, q_dec, dec_bias_lanes, msuf, mtot, cache_k, cache_v)


def _decode_units(pt_ref, q_ref, bias_ref, msuf_ref, mtot_ref, ck_ref, cv_ref, o_ref,
                  buf, sem, z_scr, a_scr, dacc, *, n_heads, n_pages, n_batch, scale):
    H = n_heads
    R = buf.shape[1]
    per_b = 2 * n_pages
    total = n_batch * per_b
    ns = DEC_SLOTS

    def copy(f, slot):
        bb = f // per_b
        jj = f % per_b
        page = pt_ref[bb, jj % n_pages]
        is_k = jj < n_pages
        return is_k, (pltpu.make_async_copy(ck_ref.at[page], buf.at[slot], sem.at[slot]),
                      pltpu.make_async_copy(cv_ref.at[page], buf.at[slot], sem.at[slot]))

    def start(f, slot):
        is_k, (ck, cv) = copy(f, slot)

        @pl.when(is_k)
        def _():
            ck.start()

        @pl.when(jnp.logical_not(is_k))
        def _():
            cv.start()

    def wait(slot):
        pltpu.make_async_copy(ck_ref.at[0], buf.at[slot], sem.at[slot]).wait()

    def prologue():
        for s in range(ns):
            start(s, s)

    def advance(f, slot):
        @pl.when(f + ns < total)
        def _():
            start(f + ns, slot)

    U = DEC_UNROLL
    group = range(U)
    n_groups = n_pages // U
    units_per_b = 2 * n_groups + 1

    def own_lanes():
        return (_iota2((H, R), 1) % H) == _iota2((H, R), 0)

    def key_unit(b, jo):
        qb = (q_ref[b] * (scale * LOG2E)).astype(bf16)
        own = own_lanes()
        j0 = pl.multiple_of(jo * U, U)
        f0 = b * per_b + j0
        slots = [(f0 + u) % ns for u in group]
        for u in group:
            wait(slots[u])
        kp = [buf[slots[u]].astype(bf16) for u in group]
        zz = [_dot_nt(qb, kp[u]) for u in group]
        rows = [jnp.sum(jnp.where(own, zz[u], 0.0), axis=0, keepdims=True) for u in group]
        z_scr[pl.ds(j0, U), :] = jnp.concatenate(rows, axis=0)
        for u in group:
            advance(f0 + u, slots[u])

    def weights_unit():
        z2 = z_scr[...] + bias_ref[...] * LOG2E
        nl = _softplus2(z2)
        nlb = nl.astype(bf16)
        s_incl = _dot(nlb, msuf_ref[...])
        tot = _dot(nlb, mtot_ref[...])
        later = jnp.where(_iota2((n_pages, n_pages), 1) > _iota2((n_pages, n_pages), 0), 1.0, 0.0).astype(bf16)
        carry = _dot3_right(later, tot)
        a_scr[...] = jnp.exp2(z2 - s_incl - carry)
        dacc[...] = jnp.zeros(dacc.shape, f32)

    def value_unit(b, jo):
        own = own_lanes()
        j0 = pl.multiple_of(jo * U, U)
        f0 = b * per_b + n_pages + j0
        slots = [(f0 + u) % ns for u in group]
        for u in group:
            wait(slots[u])
        vp = [buf[slots[u]].astype(bf16) for u in group]
        a_rows = a_scr[pl.ds(j0, U), :]
        abd = [jnp.where(own, jnp.broadcast_to(a_rows[u:u + 1, :], (H, R)), 0.0).astype(bf16) for u in group]
        parts = [_dot(abd[u], vp[u]) for u in group]
        for u in group:
            advance(f0 + u, slots[u])
        dacc[...] += sum(parts[1:], parts[0])

        @pl.when(jo == n_groups - 1)
        def _():
            o_ref[b] = dacc[...]

    def unit(u):
        b = u // units_per_b
        r = u - b * units_per_b

        @pl.when(r < n_groups)
        def _():
            key_unit(b, r)

        @pl.when(r == n_groups)
        def _():
            weights_unit()

        @pl.when(r > n_groups)
        def _():
            value_unit(b, r - (n_groups + 1))

    return prologue, unit, n_batch * units_per_b


def _dec_finish_kernel(o_ref, q_ref, k_ref, v_ref, bias_ref, gn_ref, out_ref, *, past, dec_seq, scale):
    kpos = past + _iota2(o_ref.shape, 1) * 0 + (dec_seq - 1)
    qpos = past + _iota2(o_ref.shape, 1) * 0 + (dec_seq - 1)
    valid = kpos < qpos
    z = jnp.sum(q_ref[...] * k_ref[...], axis=-1, keepdims=True) * scale + bias_ref[...]
    beta = _sigmoid(z)
    o = o_ref[...]
    o = jnp.where(valid, o * (1.0 - beta) + beta * v_ref[...], o)
    out_ref[...] = o * lax.rsqrt(jnp.mean(o * o, axis=-1, keepdims=True) + EPS) * gn_ref[...]


def _dec_finish(o_past, q_s, k_s, v_s, bias_col, sb_norm, past, dec_seq):
    n = o_past.shape[0]
    full = pl.BlockSpec((n, HEAD_DIM), lambda i: (0, 0))
    return pl.pallas_call(
        functools.partial(_dec_finish_kernel, past=past, dec_seq=dec_seq, scale=HEAD_DIM ** -0.5),
        grid=(1,),
        in_specs=[full, full, full, full, full, pl.BlockSpec((1, HEAD_DIM), lambda i: (0, 0))],
        out_specs=full,
        out_shape=jax.ShapeDtypeStruct((n, HEAD_DIM), f32),
    )(o_past, q_s, k_s, v_s, bias_col, sb_norm)


def _gdn_step_kernel(x_ref, cs_ref, cw_ref, blog_ref, alog_ref, adec_ref, dtb_ref, gn_ref, s_ref,
                     o_ref, s_out_ref, *, n_heads):
    H = n_heads
    x = x_ref[0]
    cs = cs_ref[0]
    y = x[0:3 * H] * cw_ref[CONV_W - 1]
    for i in range(CONV_W - 1):
        y = y + cs[i] * cw_ref[i]
    y = y * _sigmoid(y)
    q = y[0:H]
    k = y[H:2 * H]
    v = y[2 * H:3 * H]
    z = x[3 * H:4 * H]
    q = q * lax.rsqrt(jnp.sum(q * q, axis=-1, keepdims=True) + EPS) * (HEAD_DIM ** -0.5)
    k = k * lax.rsqrt(jnp.sum(k * k, axis=-1, keepdims=True) + EPS)
    beta = _sigmoid(blog_ref[0])
    g = -jnp.exp(alog_ref[...]) * _softplus(adec_ref[0] + dtb_ref[...])
    eg = jnp.exp(g)

    eye = jnp.where(_iota2((HEAD_DIM, HEAD_DIM), 0) == _iota2((HEAD_DIM, HEAD_DIM), 1), 1.0, 0.0).astype(bf16)
    k3 = _split3(k)
    q3 = _split3(q)
    kcol = _dot_nt(eye, k3[0]) + _dot_nt(eye, k3[1]) + _dot_nt(eye, k3[2])
    qcol = _dot_nt(eye, q3[0]) + _dot_nt(eye, q3[1]) + _dot_nt(eye, q3[2])

    outs = []
    for h in range(H):
        s = s_ref[0, h]
        kc = kcol[:, h:h + 1]
        egh = eg[h:h + 1, :]
        ks = jnp.sum(kc * s, axis=0, keepdims=True)
        vn = (v[h:h + 1, :] - egh * ks) * beta[h:h + 1, :]
        s_new = s * egh + kc * vn
        s_out_ref[0, h] = s_new
        outs.append(jnp.sum(qcol[:, h:h + 1] * s_new, axis=0, keepdims=True))
    o = jnp.concatenate(outs, axis=0)
    o = o * lax.rsqrt(jnp.mean(o * o, axis=-1, keepdims=True) + EPS) * gn_ref[...]
    o_ref[0] = o * (z * _sigmoid(z))


def _gdn_step(x_s, conv_state, conv_w, blog, adec, alog, dtb, gdn_norm, state, n_heads):
    n_batch = x_s.shape[0]
    H = n_heads
    const2 = lambda b: (0, 0)
    return pl.pallas_call(
        functools.partial(_gdn_step_kernel, n_heads=n_heads),
        grid=(n_batch,),
        in_specs=[
            pl.BlockSpec((1, 4 * H, HEAD_DIM), lambda b: (b, 0, 0)),
            pl.BlockSpec((1, CONV_W - 1, 3 * H, HEAD_DIM), lambda b: (b, 0, 0, 0)),
            pl.BlockSpec((CONV_W, 3 * H, HEAD_DIM), lambda b: (0, 0, 0)),
            pl.BlockSpec((1, H, LANES), lambda b: (b, 0, 0)),
            pl.BlockSpec((H, LANES), const2),
            pl.BlockSpec((1, H, LANES), lambda b: (b, 0, 0)),
            pl.BlockSpec((H, LANES), const2),
            pl.BlockSpec((1, HEAD_DIM), const2),
            pl.BlockSpec((1, H, HEAD_DIM, HEAD_DIM), lambda b: (b, 0, 0, 0)),
        ],
        out_specs=[
            pl.BlockSpec((1, H, HEAD_DIM), lambda b: (b, 0, 0)),
            pl.BlockSpec((1, H, HEAD_DIM, HEAD_DIM), lambda b: (b, 0, 0, 0)),
        ],
        out_shape=[
            jax.ShapeDtypeStruct((n_batch, H, HEAD_DIM), f32),
            jax.ShapeDtypeStruct((n_batch, H, HEAD_DIM, HEAD_DIM), f32),
        ],
        compiler_params=pltpu.CompilerParams(dimension_semantics=("parallel",)),
    )(x_s, conv_state, conv_w, blog, alog, adec, dtb, gdn_norm, state)


def _place_rows_kernel(a_s_ref, b_s_ref, a_in_ref, b_in_ref, a_ref, b_ref):
    del a_in_ref, b_in_ref
    a_ref[...] = a_s_ref[...]
    b_ref[...] = b_s_ref[...]


def _place_rows(a, b, a_rows, b_rows, row0):
    n, width = a_rows.shape
    rb = SUBLANES
    assert row0 % rb == 0 and n % rb == 0
    small = pl.BlockSpec((rb, width), lambda i: (i, 0))
    big = pl.BlockSpec((rb, width), lambda i: (row0 // rb + i, 0))
    return pl.pallas_call(
        _place_rows_kernel,
        grid=(n // rb,),
        in_specs=[small, small, pl.BlockSpec(memory_space=pl.ANY), pl.BlockSpec(memory_space=pl.ANY)],
        out_specs=[big, big],
        out_shape=[jax.ShapeDtypeStruct(a.shape, a.dtype), jax.ShapeDtypeStruct(b.shape, b.dtype)],
        input_output_aliases={2: 0, 3: 1},
    )(a_rows, b_rows, a, b)


def _outproj_kernel(x_ref, oa_ref, ob_ref, wa_ref, wb_ref, g_ref, x1_ref, hn_ref):
    x1 = x_ref[...] + _dot(oa_ref[...].astype(bf16), wa_ref[...]) + _dot(ob_ref[...].astype(bf16), wb_ref[...])
    x1_ref[...] = x1
    hn = x1 * lax.rsqrt(jnp.mean(x1 * x1, axis=-1, keepdims=True) + EPS) * g_ref[...]
    hn_ref[...] = hn.astype(bf16)


def _outproj(x_all, oa, ob, w_a, w_b, norm_ffn):
    t_pad, d = x_all.shape
    hd = oa.shape[1]
    tm = TM_OUT
    return pl.pallas_call(
        _outproj_kernel,
        grid=(t_pad // tm,),
        in_specs=[
            pl.BlockSpec((tm, d), lambda i: (i, 0)),
            pl.BlockSpec((tm, hd), lambda i: (i, 0)),
            pl.BlockSpec((tm, hd), lambda i: (i, 0)),
            pl.BlockSpec((hd, d), lambda i: (0, 0)),
            pl.BlockSpec((hd, d), lambda i: (0, 0)),
            pl.BlockSpec((1, d), lambda i: (0, 0)),
        ],
        out_specs=[pl.BlockSpec((tm, d), lambda i: (i, 0)), pl.BlockSpec((tm, d), lambda i: (i, 0))],
        out_shape=[jax.ShapeDtypeStruct((t_pad, d), f32), jax.ShapeDtypeStruct((t_pad, d), bf16)],
        compiler_params=pltpu.CompilerParams(
            dimension_semantics=("parallel",), vmem_limit_bytes=VMEM_LIMIT),
    )(x_all, oa, ob, w_a, w_b, norm_ffn)


def _ffn_kernel(x1_ref, hn_ref, wg_ref, wu_ref, wd_ref, y_ref):
    f = pl.program_id(1)

    @pl.when(f == 0)
    def _():
        y_ref[...] = x1_ref[...]

    hn = hn_ref[...]
    gate = _dot(hn, wg_ref[...])
    up = _dot(hn, wu_ref[...])
    act = (gate * _sigmoid(gate) * up).astype(bf16)
    y_ref[...] += _dot(act, wd_ref[...])


def _ffn(x1, hn, w_gate, w_up, w_down):
    t_pad, d = x1.shape
    d_ff = w_gate.shape[1]
    tm, tf = TM_PROJ, TF_FFN
    assert d_ff % tf == 0
    return pl.pallas_call(
        _ffn_kernel,
        grid=(t_pad // tm, d_ff // tf),
        in_specs=[
            pl.BlockSpec((tm, d), lambda i, f: (i, 0)),
            pl.BlockSpec((tm, d), lambda i, f: (i, 0)),
            pl.BlockSpec((d, tf), lambda i, f: (0, f)),
            pl.BlockSpec((d, tf), lambda i, f: (0, f)),
            pl.BlockSpec((tf, d), lambda i, f: (f, 0)),
        ],
        out_specs=pl.BlockSpec((tm, d), lambda i, f: (i, 0)),
        out_shape=jax.ShapeDtypeStruct((t_pad, d), f32),
        compiler_params=pltpu.CompilerParams(
            dimension_semantics=("parallel", "arbitrary"), vmem_limit_bytes=VMEM_LIMIT),
    )(x1, hn, w_gate, w_up, w_down)


def _lane_pad(v, offset):
    return jnp.zeros((1, LANES), f32).at[0, offset:offset + v.shape[0]].set(v)


def kernel(x_prompt, x_sample, cache_k, cache_v, state_gdn, state_conv, page_table, meta, norm_mix, w_in,
           conv_w, a_log, dt_bias, gdn_norm, q_norm, k_norm, sb_norm, sb_bias, w_out, norm_ffn, w_gate,
           w_up, w_down):
    depth = w_in.shape[0]
    bsz, seq, d = x_prompt.shape
    dbsz, dec_seq, _ = x_sample.shape
    assert depth == 1 and bsz == 1 and dec_seq == 1
    H = d // (2 * HEAD_DIM)
    hd = H * HEAD_DIM
    n_pages, page_size = page_table.shape[1], cache_k.shape[2]
    past = n_pages * page_size
    t_valid = N_META + seq
    t_used = t_valid + dbsz
    t_pad = -(-t_used // ROW_ALIGN) * ROW_ALIGN
    l = 0

    x_all = jnp.concatenate(
        [meta.astype(f32), x_prompt[0], x_sample[:, 0], jnp.zeros((t_pad - t_used, d), f32)], axis=0)

    w = w_in[l]
    o_z = 3 * hd
    o_b = o_z + hd
    o_q = o_b + 2 * H
    w16 = w.astype(bf16)
    w_sec = jnp.concatenate([w16[:, :o_b], w16[:, o_q:]], axis=1)
    w_ba = jnp.concatenate([w16[:, o_b:o_q], jnp.zeros((d, LANES - 2 * H), bf16)], axis=1)

    p, ba, bat, kvb, kf, vf = _inproj(x_all, norm_mix[l][None], w_sec, w_ba, w_ba.T, q_norm[l][None],
                                      k_norm[l][None], H, t_valid)

    cw = conv_w[l]
    alog_r = _lane_pad(a_log[l], H)
    dt_r = _lane_pad(dt_bias[l], H)
    oa, s_fin = _gdn_prompt(p, ba, bat, cw, alog_r, dt_r,
                            jnp.broadcast_to(alog_r.T, (LANES, LANES)), jnp.broadcast_to(dt_r.T, (LANES, LANES)),
                            gdn_norm[l][None], H, t_valid)

    ps_rows = p[t_valid:t_used]
    ba_s = ba[t_valid:t_used]
    x_s = ps_rows[:, :4 * hd].reshape(dbsz, 4 * H, HEAD_DIM)
    blog = jnp.broadcast_to(ba_s[:, :H, None], (dbsz, H, LANES))
    adec = jnp.broadcast_to(ba_s[:, H:2 * H, None], (dbsz, H, LANES))
    oa_s, s_new = _gdn_step(
        x_s, state_conv[l].reshape(dbsz, CONV_W - 1, 3 * H, HEAD_DIM), cw.reshape(CONV_W, 3 * H, HEAD_DIM),
        blog, adec, jnp.broadcast_to(a_log[l][:, None], (H, LANES)),
        jnp.broadcast_to(dt_bias[l][:, None], (H, LANES)), gdn_norm[l][None], state_gdn[l], H)

    q_s = ps_rows[:, 4 * hd:5 * hd]
    k_s = ps_rows[:, 5 * hd:6 * hd]
    v_s = ps_rows[:, 6 * hd:7 * hd]
    R = page_size * H
    ridx = jnp.arange(R)
    same_head = (ridx[:, None] % H) == (ridx[None, :] % H)
    msuf = jnp.logical_and(same_head, (ridx[:, None] // H) >= (ridx[None, :] // H)).astype(bf16)
    mtot = same_head.astype(bf16)
    bias_lanes = jnp.tile(sb_bias[l], page_size)[None]
    ob, o_past = _sb_attn_prompt_and_decode(
        p, kvb, sb_bias[l], sb_norm[l][None], page_table, q_s.reshape(dbsz, H, HEAD_DIM), bias_lanes, msuf, mtot,
        cache_k[l].reshape(-1, R, HEAD_DIM), cache_v[l].reshape(-1, R, HEAD_DIM), H)
    bias_col = jnp.broadcast_to(jnp.tile(sb_bias[l], dbsz)[:, None], (dbsz * H, HEAD_DIM))
    ob_s = _dec_finish(o_past.reshape(dbsz * H, HEAD_DIM), q_s.reshape(dbsz * H, HEAD_DIM),
                       k_s.reshape(dbsz * H, HEAD_DIM), v_s.reshape(dbsz * H, HEAD_DIM),
                       bias_col, sb_norm[l][None], past, dec_seq)

    oa, ob = _place_rows(oa, ob, oa_s.reshape(dbsz, hd), ob_s.reshape(dbsz, hd), t_valid)

    wo = w_out[l].astype(bf16)
    x1, hn = _outproj(x_all, oa, ob, wo[:hd], wo[hd:], norm_ffn[l][None])
    y = _ffn(x1, hn, w_gate[l].astype(bf16), w_up[l].astype(bf16), w_down[l].astype(bf16))

    new_conv_s = jnp.concatenate([state_conv[l][:, 1:], ps_rows[:, None, :3 * hd]], axis=1)
    return (
        y[N_META:t_valid][None],
        y[t_valid:t_used][:, None],
        kf.reshape(1, 1, t_valid, H, HEAD_DIM),
        vf.reshape(1, 1, t_valid, H, HEAD_DIM),
        s_fin[None, None],
        p[t_valid - (CONV_W - 1):t_valid, :3 * hd][None, None],
        k_s.reshape(1, dbsz, 1, H, HEAD_DIM),
        v_s.reshape(1, dbsz, 1, H, HEAD_DIM),
        s_new[None],
        new_conv_s[None],
    )
```

```python
import functools

import jax
import jax.numpy as jnp
from jax import lax
from jax.experimental import pallas as pl
from jax.experimental.pallas import tpu as pltpu

EPS = 1e-6
N_META = 16
HEAD_DIM = 128
CONV_W = 4
LOG2E = 1.4426950408889634
LN2 = 0.6931471805599453

SUBLANES = 8
LANES = 128
VMEM_LIMIT = 56 * 1024 * 1024

ROW_ALIGN = 768
TM_PROJ = 768
TM_OUT = 384
TF_FFN = 512
GDN_CHUNK = 128
TQ_ATT = 768
TK_ATT = 768
ATT_GROUP = 256
ATT_HEADS = 2
DEC_SLOTS = 32
DEC_UNROLL = 8

f32 = jnp.float32
bf16 = jnp.bfloat16


def _dot(a, b):
    return jnp.dot(a, b, preferred_element_type=f32)


def _dot_nt(a, b):
    return lax.dot_general(a, b, (((1,), (1,)), ((), ())), preferred_element_type=f32)


def _dot_tn(a, b):
    return lax.dot_general(a, b, (((0,), (0,)), ((), ())), preferred_element_type=f32)


def _split3(x):
    hi = x.astype(bf16)
    r = x - hi.astype(f32)
    mid = r.astype(bf16)
    lo = (r - mid.astype(f32)).astype(bf16)
    return hi, mid, lo


def _dot3_left(x, m_bf16):
    hi, mid, lo = _split3(x)
    return _dot(hi, m_bf16) + _dot(mid, m_bf16) + _dot(lo, m_bf16)


def _dot3_right(m_bf16, x):
    hi, mid, lo = _split3(x)
    return _dot(m_bf16, hi) + _dot(m_bf16, mid) + _dot(m_bf16, lo)


def _sigmoid(x):
    return 1.0 / (1.0 + jnp.exp(-x))


def _softplus(x):
    return jnp.maximum(x, 0.0) + jnp.log(1.0 + jnp.exp(-jnp.abs(x)))


def _softplus2(z2):
    return jnp.maximum(z2, 0.0) + jnp.log2(1.0 + jnp.exp2(-jnp.abs(z2)))


INV_BASE = 16


def _unit_lower_inverse_minus_eye(mats, ii, jj):
    C = mats[0].shape[0]
    idx = range(len(mats))
    diag = (ii // INV_BASE) == (jj // INV_BASE)
    d = [jnp.where(diag, a, 0.0) for a in mats]
    n = [-x for x in d]
    p = d
    for _ in range(INV_BASE.bit_length() - 2):
        pb = [x.astype(bf16) for x in p]
        p = [_dot(x, x) for x in pb]
        pb = [x.astype(bf16) for x in p]
        n = [n[i] + p[i] + _dot(n[i].astype(bf16), pb[i]) for i in idx]
    bs = INV_BASE
    while bs < C:
        pair = jnp.logical_and((ii // (2 * bs)) == (jj // (2 * bs)), (ii // bs) != (jj // bs))
        off = [jnp.where(pair, a, 0.0) for a in mats]
        x = [off[i] + _dot(off[i].astype(bf16), n[i].astype(bf16)) for i in idx]
        n = [n[i] - (x[i] + _dot(n[i].astype(bf16), x[i].astype(bf16))) for i in idx]
        bs *= 2
    return n


def _iota2(shape, dim):
    return lax.broadcasted_iota(jnp.int32, shape, dim)


def _inproj_kernel(x_ref, g_ref, w_ref, wba_ref, wbat_ref, qn_ref, kn_ref,
                   p_ref, ba_ref, bat_ref, kvb_ref, kf_ref, vf_ref, xn_scr, *, n_heads):
    j = pl.program_id(1)

    @pl.when(j == 0)
    def _():
        x = x_ref[...]
        xn = x * lax.rsqrt(jnp.mean(x * x, axis=-1, keepdims=True) + EPS) * g_ref[...]
        xnb = xn.astype(bf16)
        xn_scr[...] = xnb
        ba_ref[...] = _dot(xnb, wba_ref[...])
        bat_ref[...] = _dot_nt(wbat_ref[...], xnb)

    y = _dot(xn_scr[...], w_ref[...])

    def headnorm(gain_ref, copy_ref, rows_ref):
        for h in range(n_heads):
            sl = slice(h * HEAD_DIM, (h + 1) * HEAD_DIM)
            yh = y[:, sl]
            ms = jnp.mean(yh * yh, axis=-1, keepdims=True)
            yn = yh * lax.rsqrt(ms + EPS) * gain_ref[...]
            p_ref[:, sl] = yn
            if copy_ref is not None:
                copy_ref[:, sl] = yn.astype(bf16)
                rows_ref[:, sl] = yn

    @pl.when(j == 4)
    def _():
        headnorm(qn_ref, None, None)

    @pl.when(j == 5)
    def _():
        headnorm(kn_ref, kvb_ref, kf_ref)

    @pl.when(j == 6)
    def _():
        p_ref[...] = y
        kvb_ref[...] = y.astype(bf16)
        vf_ref[...] = y

    @pl.when(j < 4)
    def _():
        p_ref[...] = y


def _inproj(x_all, norm_mix, w_sec, w_ba, w_bat, q_norm, k_norm, n_heads, t_valid):
    t_pad, d = x_all.shape
    hd = n_heads * HEAD_DIM
    tm = TM_PROJ
    grid = (t_pad // tm, 7)
    return pl.pallas_call(
        functools.partial(_inproj_kernel, n_heads=n_heads),
        grid=grid,
        in_specs=[
            pl.BlockSpec((tm, d), lambda i, j: (i, 0)),
            pl.BlockSpec((1, d), lambda i, j: (0, 0)),
            pl.BlockSpec((d, hd), lambda i, j: (0, j)),
            pl.BlockSpec((d, LANES), lambda i, j: (0, 0)),
            pl.BlockSpec((LANES, d), lambda i, j: (0, 0)),
            pl.BlockSpec((1, HEAD_DIM), lambda i, j: (0, 0)),
            pl.BlockSpec((1, HEAD_DIM), lambda i, j: (0, 0)),
        ],
        out_specs=[
            pl.BlockSpec((tm, hd), lambda i, j: (i, j)),
            pl.BlockSpec((tm, LANES), lambda i, j: (i, 0)),
            pl.BlockSpec((LANES, tm), lambda i, j: (0, i)),
            pl.BlockSpec((tm, hd), lambda i, j: (i, jnp.clip(j - 5, 0, 1))),
            pl.BlockSpec((tm, hd), lambda i, j: (i, 0)),
            pl.BlockSpec((tm, hd), lambda i, j: (i, 0)),
        ],
        out_shape=[
            jax.ShapeDtypeStruct((t_pad, 7 * hd), f32),
            jax.ShapeDtypeStruct((t_pad, LANES), f32),
            jax.ShapeDtypeStruct((LANES, t_pad), f32),
            jax.ShapeDtypeStruct((t_pad, 2 * hd), bf16),
            jax.ShapeDtypeStruct((t_valid, hd), f32),
            jax.ShapeDtypeStruct((t_valid, hd), f32),
        ],
        scratch_shapes=[pltpu.VMEM((tm, d), bf16)],
        compiler_params=pltpu.CompilerParams(
            dimension_semantics=("parallel", "arbitrary"), vmem_limit_bytes=VMEM_LIMIT),
    )(x_all, norm_mix, w_sec, w_ba, w_bat, q_norm, k_norm)


def _gdn_kernel(x_ref, ba_ref, bat_ref, cw_ref, alog_r_ref, dt_r_ref, alog_c_ref, dt_c_ref, gn_ref,
                oa_ref, s_ref, xbuf, *, n_heads, t_valid):
    c = pl.program_id(0)
    C = GDN_CHUNK
    hd = n_heads * HEAD_DIM
    hist = CONV_W - 1

    @pl.when(c == 0)
    def _():
        xbuf[0:SUBLANES, :] = jnp.zeros((SUBLANES, 3 * hd), f32)
        s_ref[...] = jnp.zeros(s_ref.shape, f32)

    xbuf[SUBLANES:SUBLANES + C, :] = x_ref[:, 0:3 * hd]
    y = xbuf[SUBLANES:SUBLANES + C, :] * cw_ref[hist:hist + 1, :]
    for i in range(hist):
        y = y + xbuf[SUBLANES - hist + i:SUBLANES - hist + i + C, :] * cw_ref[i:i + 1, :]
    y = y * _sigmoid(y)
    xbuf[SUBLANES - hist:SUBLANES, :] = xbuf[SUBLANES + C - hist:SUBLANES + C, :]

    ii = _iota2((C, C), 0)
    jj = _iota2((C, C), 1)
    incl = ii >= jj
    strict = ii > jj
    low_ones = jnp.where(incl, 1.0, 0.0).astype(bf16)
    up_ones = jnp.where(jj >= ii, 1.0, 0.0).astype(bf16)

    row_ok = (c * C + _iota2((C, LANES), 0)) < t_valid
    nr = 2 * n_heads
    lane_ok = (c * C + _iota2((nr, C), 1)) < t_valid
    ba = ba_ref[...]
    bat = bat_ref[0:nr, :]
    beta_cols = jnp.where(row_ok, _sigmoid(ba), 0.0)
    g_cols = jnp.where(row_ok, -jnp.exp(alog_r_ref[...]) * _softplus(ba + dt_r_ref[...]), 0.0)
    g_rows = jnp.where(lane_ok, -jnp.exp(alog_c_ref[0:nr, :]) * _softplus(bat + dt_c_ref[0:nr, :]), 0.0)
    gc_cols = _dot3_right(low_ones, g_cols)
    gc_rows = _dot3_left(g_rows, up_ones)

    heads = range(n_heads)

    def head_cols(off, h):
        return slice(off + h * HEAD_DIM, off + (h + 1) * HEAD_DIM)

    q = [y[:, head_cols(0, h)] for h in heads]
    k = [y[:, head_cols(hd, h)] for h in heads]
    v = [y[:, head_cols(2 * hd, h)] for h in heads]
    q = [x * lax.rsqrt(jnp.sum(x * x, axis=-1, keepdims=True) + EPS) * (HEAD_DIM ** -0.5) for x in q]
    k = [x * lax.rsqrt(jnp.sum(x * x, axis=-1, keepdims=True) + EPS) for x in k]

    beta = [beta_cols[:, h:h + 1] for h in heads]
    gcc = [gc_cols[:, n_heads + h:n_heads + h + 1] for h in heads]
    gcr = [gc_rows[n_heads + h:n_heads + h + 1, :] for h in heads]
    g_last = [gc_cols[C - 1:C, n_heads + h:n_heads + h + 1] for h in heads]
    decay = [jnp.where(incl, jnp.exp(jnp.minimum(gcc[h] - gcr[h], 0.0)), 0.0) for h in heads]
    kb = [k[h] * beta[h] for h in heads]
    kbf = [k[h].astype(bf16) for h in heads]
    a = [jnp.where(strict, _dot_nt(kb[h].astype(bf16), kbf[h]) * decay[h], 0.0) for h in heads]
    qk = [(_dot_nt(q[h].astype(bf16), kbf[h]) * decay[h]).astype(bf16) for h in heads]

    n = _unit_lower_inverse_minus_eye(a, ii, jj)
    egc = [jnp.exp(gcc[h]) for h in heads]
    rhs_u = [v[h] * beta[h] for h in heads]
    rhs_w = [kb[h] * egc[h] for h in heads]
    nb = [n[h].astype(bf16) for h in heads]
    u = [rhs_u[h] + _dot(nb[h], rhs_u[h].astype(bf16)) for h in heads]
    w = [(rhs_w[h] + _dot(nb[h], rhs_w[h].astype(bf16))).astype(bf16) for h in heads]
    q_dec = [(q[h] * egc[h]).astype(bf16) for h in heads]
    k_dec = [(k[h] * jnp.exp(g_last[h] - gcc[h])).astype(bf16) for h in heads]

    s = [s_ref[h] for h in heads]
    sb = [s[h].astype(bf16) for h in heads]
    vnb = [(u[h] - _dot(w[h], sb[h])).astype(bf16) for h in heads]
    o = [_dot(q_dec[h], sb[h]) + _dot(qk[h], vnb[h]) for h in heads]
    for h in heads:
        s_ref[h] = s[h] * jnp.exp(g_last[h]) + _dot_tn(k_dec[h], vnb[h])
    for h in heads:
        z = x_ref[:, head_cols(3 * hd, h)]
        on = o[h] * lax.rsqrt(jnp.mean(o[h] * o[h], axis=-1, keepdims=True) + EPS) * gn_ref[...]
        oa_ref[:, head_cols(0, h)] = on * (z * _sigmoid(z))


def _gdn_prompt(p, ba, bat, conv_w, alog_r, dt_r, alog_c, dt_c, gdn_norm, n_heads, t_valid):
    t_pad = p.shape[0]
    hd = n_heads * HEAD_DIM
    C = GDN_CHUNK
    const = lambda c: (0, 0)
    return pl.pallas_call(
        functools.partial(_gdn_kernel, n_heads=n_heads, t_valid=t_valid),
        grid=(t_pad // C,),
        in_specs=[
            pl.BlockSpec((C, 4 * hd), lambda c: (c, 0)),
            pl.BlockSpec((C, LANES), lambda c: (c, 0)),
            pl.BlockSpec((LANES, C), lambda c: (0, c)),
            pl.BlockSpec((CONV_W, 3 * hd), const),
            pl.BlockSpec((1, LANES), const),
            pl.BlockSpec((1, LANES), const),
            pl.BlockSpec((LANES, LANES), const),
            pl.BlockSpec((LANES, LANES), const),
            pl.BlockSpec((1, HEAD_DIM), const),
        ],
        out_specs=[
            pl.BlockSpec((C, hd), lambda c: (c, 0)),
            pl.BlockSpec((n_heads, HEAD_DIM, HEAD_DIM), lambda c: (0, 0, 0)),
        ],
        out_shape=[
            jax.ShapeDtypeStruct((t_pad, hd), f32),
            jax.ShapeDtypeStruct((n_heads, HEAD_DIM, HEAD_DIM), f32),
        ],
        scratch_shapes=[pltpu.VMEM((SUBLANES + C, 3 * hd), f32)],
        compiler_params=pltpu.CompilerParams(
            dimension_semantics=("arbitrary",), vmem_limit_bytes=VMEM_LIMIT),
    )(p, ba, bat, conv_w, alog_r, dt_r, alog_c, dt_c, gdn_norm)


def _sb_attn_kernel(pt_ref, bias_ref, q_ref, k_ref, v_ref, gn_ref,
                    qd_ref, dbias_ref, msuf_ref, mtot_ref, ck_ref, cv_ref,
                    o_ref, od_ref,
                    acc_scr, carry_scr, buf, sem, z_scr, a_scr, dacc, cnt,
                    *, scale, n_heads, n_pages, n_batch, units_per_span):
    hp = pl.program_id(0)
    i = pl.program_id(1)
    first_step = jnp.logical_and(hp == 0, i == 0)
    last_step = jnp.logical_and(hp == pl.num_programs(0) - 1, i == pl.num_programs(1) - 1)
    dec_prologue, dec_unit, n_units = _decode_units(
        pt_ref, qd_ref, dbias_ref, msuf_ref, mtot_ref, ck_ref, cv_ref, od_ref, buf, sem, z_scr, a_scr, dacc,
        n_heads=n_heads, n_pages=n_pages, n_batch=n_batch, scale=scale)

    @pl.when(first_step)
    def _():
        cnt[0] = 0
        dec_prologue()

    def run_units(n):
        def one(_, c):
            u = cnt[0]

            @pl.when(u < n_units)
            def _():
                dec_unit(u)
                cnt[0] = u + 1

            return c

        lax.fori_loop(0, n, one, 0)

    tq, tk, G = TQ_ATT, TK_ATT, ATT_GROUP
    heads = range(ATT_HEADS)
    groups = range(tk // G)

    def hcols(hh):
        return slice(hh * HEAD_DIM, (hh + 1) * HEAD_DIM)

    lane = _iota2((tq, HEAD_DIM), 1)
    ones3 = jnp.where(lane < 3, 1.0, 0.0).astype(bf16)
    q2, k_tail = [], []
    for hh in heads:
        q2.append(jnp.concatenate([(q_ref[:, hcols(hh)] * (scale * LOG2E)).astype(bf16), ones3], axis=1))
        b_hi, b_mid, b_lo = _split3(jnp.full((tk, HEAD_DIM), bias_ref[hp * ATT_HEADS + hh] * LOG2E, f32))
        klane = _iota2((tk, HEAD_DIM), 1)
        tail = jnp.where(klane == 0, b_hi.astype(f32),
                         jnp.where(klane == 1, b_mid.astype(f32), jnp.where(klane == 2, b_lo.astype(f32), 0.0)))
        k_tail.append(tail.astype(bf16))

    mo = jnp.where(_iota2((G, G), 0) >= _iota2((G, G), 1), 1.0, 0.0).astype(bf16)

    acc_scr[...] = jnp.zeros(acc_scr.shape, f32)
    carry_scr[...] = jnp.zeros(carry_scr.shape, f32)

    def span(start, masked):
        kaug = [jnp.concatenate([k_ref[pl.ds(start, tk), hcols(hh)], k_tail[hh]], axis=1) for hh in heads]
        z2 = [_dot_nt(q2[hh], kaug[hh]) for hh in heads]
        nl = [_softplus2(z) for z in z2]
        if masked:
            valid = _iota2((tq, tk), 1) < _iota2((tq, tk), 0)
            nl = [jnp.where(valid, x, 0.0) for x in nl]
        nlb = [x.astype(bf16) for x in nl]
        carry = [carry_scr[hh, :, 0:1] for hh in heads]
        parts = [[None] * len(groups) for _ in heads]
        for g in reversed(groups):
            gs = slice(g * G, (g + 1) * G)
            for hh in heads:
                sg = _dot(nlb[hh][:, gs], mo)
                w = jnp.exp2(z2[hh][:, gs] - sg - carry[hh])
                if masked:
                    w = jnp.where(valid[:, gs], w, 0.0)
                parts[hh][g] = w.astype(bf16)
                carry[hh] = carry[hh] + sg[:, 0:1]
        for hh in heads:
            carry_scr[hh] = jnp.broadcast_to(carry[hh], (tq, LANES))
            acc_scr[hh] += _dot(jnp.concatenate(parts[hh], axis=-1), v_ref[pl.ds(start, tk), hcols(hh)])

    span(pl.multiple_of(i * tq, tq), True)
    run_units(units_per_span)

    def body(t, _):
        span(pl.multiple_of((i - 1 - t) * tk, tk), False)
        run_units(units_per_span)
        return 0

    lax.fori_loop(0, i * (tq // tk), body, 0)

    for hh in heads:
        o = acc_scr[hh]
        o_ref[:, hcols(hh)] = o * lax.rsqrt(jnp.mean(o * o, axis=-1, keepdims=True) + EPS) * gn_ref[...]

    @pl.when(last_step)
    def _():
        def rest(_, c):
            u = cnt[0]
            dec_unit(u)
            cnt[0] = u + 1
            return c

        lax.fori_loop(0, n_units - cnt[0], rest, 0)


def _sb_attn_prompt_and_decode(p, kvb, sb_bias, sb_norm, page_table, q_dec, dec_bias_lanes, msuf, mtot,
                               cache_k, cache_v, n_heads):
    t_pad = p.shape[0]
    hd = n_heads * HEAD_DIM
    tq = TQ_ATT
    wide = ATT_HEADS * HEAD_DIM
    n_batch, n_pages = page_table.shape
    R = cache_k.shape[1]
    assert TQ_ATT == TK_ATT and TK_ATT % ATT_GROUP == 0 and n_heads % ATT_HEADS == 0
    assert n_heads == SUBLANES and n_pages % DEC_UNROLL == 0 and DEC_SLOTS % DEC_UNROLL == 0
    assert (2 * n_pages) % DEC_SLOTS == 0
    sec = hd // wide
    nq = t_pad // tq
    n_spans = (n_heads // ATT_HEADS) * (nq * (nq + 1) // 2)
    n_units = n_batch * (2 * (n_pages // DEC_UNROLL) + 1)
    units_per_span = -(-n_units // n_spans)
    const2 = lambda h, i, pt: (0, 0)
    once = pl.Buffered(1)
    grid_spec = pltpu.PrefetchScalarGridSpec(
        num_scalar_prefetch=1,
        grid=(n_heads // ATT_HEADS, nq),
        in_specs=[
            pl.BlockSpec(memory_space=pltpu.SMEM),
            pl.BlockSpec((tq, wide), lambda h, i, pt: (i, 4 * sec + h)),
            pl.BlockSpec((t_pad, wide), lambda h, i, pt: (0, h), pipeline_mode=once),
            pl.BlockSpec((t_pad, wide), lambda h, i, pt: (0, sec + h), pipeline_mode=once),
            pl.BlockSpec((1, HEAD_DIM), const2),
            pl.BlockSpec((n_batch, n_heads, HEAD_DIM), lambda h, i, pt: (0, 0, 0), pipeline_mode=once),
            pl.BlockSpec((1, R), const2),
            pl.BlockSpec((R, R), const2, pipeline_mode=once),
            pl.BlockSpec((R, R), const2, pipeline_mode=once),
            pl.BlockSpec(memory_space=pl.ANY),
            pl.BlockSpec(memory_space=pl.ANY),
        ],
        out_specs=[
            pl.BlockSpec((tq, wide), lambda h, i, pt: (i, h)),
            pl.BlockSpec((n_batch, n_heads, HEAD_DIM), lambda h, i, pt: (0, 0, 0)),
        ],
        scratch_shapes=[
            pltpu.VMEM((ATT_HEADS, tq, HEAD_DIM), f32),
            pltpu.VMEM((ATT_HEADS, tq, LANES), f32),
            pltpu.VMEM((DEC_SLOTS, R, HEAD_DIM), f32),
            pltpu.SemaphoreType.DMA((DEC_SLOTS,)),
            pltpu.VMEM((n_pages, R), f32),
            pltpu.VMEM((n_pages, R), f32),
            pltpu.VMEM((n_heads, HEAD_DIM), f32),
            pltpu.SMEM((1,), jnp.int32),
        ],
    )
    return pl.pallas_call(
        functools.partial(_sb_attn_kernel, scale=HEAD_DIM ** -0.5, n_heads=n_heads, n_pages=n_pages,
                          n_batch=n_batch, units_per_span=units_per_span),
        grid_spec=grid_spec,
        out_shape=[jax.ShapeDtypeStruct((t_pad, hd), f32),
                   jax.ShapeDtypeStruct((n_batch, n_heads, HEAD_DIM), f32)],
        compiler_params=pltpu.CompilerParams(
            dimension_semantics=("arbitrary", "arbitrary"), vmem_limit_bytes=VMEM_LIMIT),
    )(page_table, sb_bias, p, kvb, kvb, sb_norm, q_dec, dec_bias_lanes, msuf, mtot, cache_k, cache_v)


def _decode_units(pt_ref, q_ref, bias_ref, msuf_ref, mtot_ref, ck_ref, cv_ref, o_ref,
                  buf, sem, z_scr, a_scr, dacc, *, n_heads, n_pages, n_batch, scale):
    H = n_heads
    R = buf.shape[1]
    per_b = 2 * n_pages
    total = n_batch * per_b
    ns = DEC_SLOTS

    def copy(f, slot):
        bb = f // per_b
        jj = f % per_b
        page = pt_ref[bb, jj % n_pages]
        is_k = jj < n_pages
        return is_k, (pltpu.make_async_copy(ck_ref.at[page], buf.at[slot], sem.at[slot]),
                      pltpu.make_async_copy(cv_ref.at[page], buf.at[slot], sem.at[slot]))

    def start(f, slot, prio):
        is_k, (ck, cv) = copy(f, slot)

        @pl.when(is_k)
        def _():
            ck.start(priority=prio)

        @pl.when(jnp.logical_not(is_k))
        def _():
            cv.start(priority=prio)

    def wait(slot):
        pltpu.make_async_copy(ck_ref.at[0], buf.at[slot], sem.at[slot]).wait()

    def prologue():
        for s in range(ns):
            start(s, s, s % 2)

    def advance(f, slot, prio):
        @pl.when(f + ns < total)
        def _():
            start(f + ns, slot, prio)

    U = DEC_UNROLL
    group = range(U)
    n_groups = n_pages // U
    units_per_b = 2 * n_groups + 1

    def own_lanes():
        return (_iota2((H, R), 1) % H) == _iota2((H, R), 0)

    def key_unit(b, jo):
        qb = (q_ref[b] * (scale * LOG2E)).astype(bf16)
        own = own_lanes()
        j0 = pl.multiple_of(jo * U, U)
        f0 = b * per_b + j0
        slots = [(f0 + u) % ns for u in group]
        for u in group:
            wait(slots[u])
        kp = [buf[slots[u]].astype(bf16) for u in group]
        zz = [_dot_nt(qb, kp[u]) for u in group]
        rows = [jnp.sum(jnp.where(own, zz[u], 0.0), axis=0, keepdims=True) for u in group]
        z_scr[pl.ds(j0, U), :] = jnp.concatenate(rows, axis=0)
        for u in group:
            advance(f0 + u, slots[u], u % 2)

    def weights_unit():
        z2 = z_scr[...] + bias_ref[...] * LOG2E
        nl = _softplus2(z2)
        nlb = nl.astype(bf16)
        s_incl = _dot(nlb, msuf_ref[...])
        tot = _dot(nlb, mtot_ref[...])
        later = jnp.where(_iota2((n_pages, n_pages), 1) > _iota2((n_pages, n_pages), 0), 1.0, 0.0).astype(bf16)
        carry = _dot3_right(later, tot)
        a_scr[...] = jnp.exp2(z2 - s_incl - carry)
        dacc[...] = jnp.zeros(dacc.shape, f32)

    def value_unit(b, jo):
        own = own_lanes()
        j0 = pl.multiple_of(jo * U, U)
        f0 = b * per_b + n_pages + j0
        slots = [(f0 + u) % ns for u in group]
        for u in group:
            wait(slots[u])
        vp = [buf[slots[u]].astype(bf16) for u in group]
        a_rows = a_scr[pl.ds(j0, U), :]
        abd = [jnp.where(own, jnp.broadcast_to(a_rows[u:u + 1, :], (H, R)), 0.0).astype(bf16) for u in group]
        parts = [_dot(abd[u], vp[u]) for u in group]
        for u in group:
            advance(f0 + u, slots[u], u % 2)
        dacc[...] += sum(parts[1:], parts[0])

        @pl.when(jo == n_groups - 1)
        def _():
            o_ref[b] = dacc[...]

    def unit(u):
        b = u // units_per_b
        r = u - b * units_per_b

        @pl.when(r < n_groups)
        def _():
            key_unit(b, r)

        @pl.when(r == n_groups)
        def _():
            weights_unit()

        @pl.when(r > n_groups)
        def _():
            value_unit(b, r - (n_groups + 1))

    return prologue, unit, n_batch * units_per_b


def _dec_finish_kernel(o_ref, q_ref, k_ref, v_ref, bias_ref, gn_ref, out_ref, *, past, dec_seq, scale):
    kpos = past + _iota2(o_ref.shape, 1) * 0 + (dec_seq - 1)
    qpos = past + _iota2(o_ref.shape, 1) * 0 + (dec_seq - 1)
    valid = kpos < qpos
    z = jnp.sum(q_ref[...] * k_ref[...], axis=-1, keepdims=True) * scale + bias_ref[...]
    beta = _sigmoid(z)
    o = o_ref[...]
    o = jnp.where(valid, o * (1.0 - beta) + beta * v_ref[...], o)
    out_ref[...] = o * lax.rsqrt(jnp.mean(o * o, axis=-1, keepdims=True) + EPS) * gn_ref[...]


def _dec_finish(o_past, q_s, k_s, v_s, bias_col, sb_norm, past, dec_seq):
    n = o_past.shape[0]
    full = pl.BlockSpec((n, HEAD_DIM), lambda i: (0, 0))
    return pl.pallas_call(
        functools.partial(_dec_finish_kernel, past=past, dec_seq=dec_seq, scale=HEAD_DIM ** -0.5),
        grid=(1,),
        in_specs=[full, full, full, full, full, pl.BlockSpec((1, HEAD_DIM), lambda i: (0, 0))],
        out_specs=full,
        out_shape=jax.ShapeDtypeStruct((n, HEAD_DIM), f32),
    )(o_past, q_s, k_s, v_s, bias_col, sb_norm)


def _gdn_step_kernel(x_ref, cs_ref, cw_ref, blog_ref, alog_ref, adec_ref, dtb_ref, gn_ref, s_ref,
                     o_ref, s_out_ref, *, n_heads):
    H = n_heads
    x = x_ref[0]
    cs = cs_ref[0]
    y = x[0:3 * H] * cw_ref[CONV_W - 1]
    for i in range(CONV_W - 1):
        y = y + cs[i] * cw_ref[i]
    y = y * _sigmoid(y)
    q = y[0:H]
    k = y[H:2 * H]
    v = y[2 * H:3 * H]
    z = x[3 * H:4 * H]
    q = q * lax.rsqrt(jnp.sum(q * q, axis=-1, keepdims=True) + EPS) * (HEAD_DIM ** -0.5)
    k = k * lax.rsqrt(jnp.sum(k * k, axis=-1, keepdims=True) + EPS)
    beta = _sigmoid(blog_ref[0])
    g = -jnp.exp(alog_ref[...]) * _softplus(adec_ref[0] + dtb_ref[...])
    eg = jnp.exp(g)

    eye = jnp.where(_iota2((HEAD_DIM, HEAD_DIM), 0) == _iota2((HEAD_DIM, HEAD_DIM), 1), 1.0, 0.0).astype(bf16)
    k3 = _split3(k)
    q3 = _split3(q)
    kcol = _dot_nt(eye, k3[0]) + _dot_nt(eye, k3[1]) + _dot_nt(eye, k3[2])
    qcol = _dot_nt(eye, q3[0]) + _dot_nt(eye, q3[1]) + _dot_nt(eye, q3[2])

    outs = []
    for h in range(H):
        s = s_ref[0, h]
        kc = kcol[:, h:h + 1]
        egh = eg[h:h + 1, :]
        ks = jnp.sum(kc * s, axis=0, keepdims=True)
        vn = (v[h:h + 1, :] - egh * ks) * beta[h:h + 1, :]
        s_new = s * egh + kc * vn
        s_out_ref[0, h] = s_new
        outs.append(jnp.sum(qcol[:, h:h + 1] * s_new, axis=0, keepdims=True))
    o = jnp.concatenate(outs, axis=0)
    o = o * lax.rsqrt(jnp.mean(o * o, axis=-1, keepdims=True) + EPS) * gn_ref[...]
    o_ref[0] = o * (z * _sigmoid(z))


def _gdn_step(x_s, conv_state, conv_w, blog, adec, alog, dtb, gdn_norm, state, n_heads):
    n_batch = x_s.shape[0]
    H = n_heads
    const2 = lambda b: (0, 0)
    return pl.pallas_call(
        functools.partial(_gdn_step_kernel, n_heads=n_heads),
        grid=(n_batch,),
        in_specs=[
            pl.BlockSpec((1, 4 * H, HEAD_DIM), lambda b: (b, 0, 0)),
            pl.BlockSpec((1, CONV_W - 1, 3 * H, HEAD_DIM), lambda b: (b, 0, 0, 0)),
            pl.BlockSpec((CONV_W, 3 * H, HEAD_DIM), lambda b: (0, 0, 0)),
            pl.BlockSpec((1, H, LANES), lambda b: (b, 0, 0)),
            pl.BlockSpec((H, LANES), const2),
            pl.BlockSpec((1, H, LANES), lambda b: (b, 0, 0)),
            pl.BlockSpec((H, LANES), const2),
            pl.BlockSpec((1, HEAD_DIM), const2),
            pl.BlockSpec((1, H, HEAD_DIM, HEAD_DIM), lambda b: (b, 0, 0, 0)),
        ],
        out_specs=[
            pl.BlockSpec((1, H, HEAD_DIM), lambda b: (b, 0, 0)),
            pl.BlockSpec((1, H, HEAD_DIM, HEAD_DIM), lambda b: (b, 0, 0, 0)),
        ],
        out_shape=[
            jax.ShapeDtypeStruct((n_batch, H, HEAD_DIM), f32),
            jax.ShapeDtypeStruct((n_batch, H, HEAD_DIM, HEAD_DIM), f32),
        ],
        compiler_params=pltpu.CompilerParams(dimension_semantics=("parallel",)),
    )(x_s, conv_state, conv_w, blog, alog, adec, dtb, gdn_norm, state)


def _place_rows_kernel(a_s_ref, b_s_ref, a_in_ref, b_in_ref, a_ref, b_ref):
    del a_in_ref, b_in_ref
    a_ref[...] = a_s_ref[...]
    b_ref[...] = b_s_ref[...]


def _place_rows(a, b, a_rows, b_rows, row0):
    n, width = a_rows.shape
    rb = SUBLANES
    assert row0 % rb == 0 and n % rb == 0
    small = pl.BlockSpec((rb, width), lambda i: (i, 0))
    big = pl.BlockSpec((rb, width), lambda i: (row0 // rb + i, 0))
    return pl.pallas_call(
        _place_rows_kernel,
        grid=(n // rb,),
        in_specs=[small, small, pl.BlockSpec(memory_space=pl.ANY), pl.BlockSpec(memory_space=pl.ANY)],
        out_specs=[big, big],
        out_shape=[jax.ShapeDtypeStruct(a.shape, a.dtype), jax.ShapeDtypeStruct(b.shape, b.dtype)],
        input_output_aliases={2: 0, 3: 1},
    )(a_rows, b_rows, a, b)


def _outproj_kernel(x_ref, oa_ref, ob_ref, wa_ref, wb_ref, g_ref, x1_ref, hn_ref):
    x1 = x_ref[...] + _dot(oa_ref[...].astype(bf16), wa_ref[...]) + _dot(ob_ref[...].astype(bf16), wb_ref[...])
    x1_ref[...] = x1
    hn = x1 * lax.rsqrt(jnp.mean(x1 * x1, axis=-1, keepdims=True) + EPS) * g_ref[...]
    hn_ref[...] = hn.astype(bf16)


def _outproj(x_all, oa, ob, w_a, w_b, norm_ffn):
    t_pad, d = x_all.shape
    hd = oa.shape[1]
    tm = TM_OUT
    return pl.pallas_call(
        _outproj_kernel,
        grid=(t_pad // tm,),
        in_specs=[
            pl.BlockSpec((tm, d), lambda i: (i, 0)),
            pl.BlockSpec((tm, hd), lambda i: (i, 0)),
            pl.BlockSpec((tm, hd), lambda i: (i, 0)),
            pl.BlockSpec((hd, d), lambda i: (0, 0)),
            pl.BlockSpec((hd, d), lambda i: (0, 0)),
            pl.BlockSpec((1, d), lambda i: (0, 0)),
        ],
        out_specs=[pl.BlockSpec((tm, d), lambda i: (i, 0)), pl.BlockSpec((tm, d), lambda i: (i, 0))],
        out_shape=[jax.ShapeDtypeStruct((t_pad, d), f32), jax.ShapeDtypeStruct((t_pad, d), bf16)],
        compiler_params=pltpu.CompilerParams(
            dimension_semantics=("parallel",), vmem_limit_bytes=VMEM_LIMIT),
    )(x_all, oa, ob, w_a, w_b, norm_ffn)


def _ffn_kernel(x1_ref, hn_ref, wg_ref, wu_ref, wd_ref, y_ref):
    f = pl.program_id(1)

    @pl.when(f == 0)
    def _():
        y_ref[...] = x1_ref[...]

    hn = hn_ref[...]
    gate = _dot(hn, wg_ref[...])
    up = _dot(hn, wu_ref[...])
    act = (gate * _sigmoid(gate) * up).astype(bf16)
    y_ref[...] += _dot(act, wd_ref[...])


def _ffn(x1, hn, w_gate, w_up, w_down):
    t_pad, d = x1.shape
    d_ff = w_gate.shape[1]
    tm, tf = TM_PROJ, TF_FFN
    assert d_ff % tf == 0
    return pl.pallas_call(
        _ffn_kernel,
        grid=(t_pad // tm, d_ff // tf),
        in_specs=[
            pl.BlockSpec((tm, d), lambda i, f: (i, 0)),
            pl.BlockSpec((tm, d), lambda i, f: (i, 0)),
            pl.BlockSpec((d, tf), lambda i, f: (0, f)),
            pl.BlockSpec((d, tf), lambda i, f: (0, f)),
            pl.BlockSpec((tf, d), lambda i, f: (f, 0)),
        ],
        out_specs=pl.BlockSpec((tm, d), lambda i, f: (i, 0)),
        out_shape=jax.ShapeDtypeStruct((t_pad, d), f32),
        compiler_params=pltpu.CompilerParams(
            dimension_semantics=("parallel", "arbitrary"), vmem_limit_bytes=VMEM_LIMIT),
    )(x1, hn, w_gate, w_up, w_down)


def _lane_pad(v, offset):
    return jnp.zeros((1, LANES), f32).at[0, offset:offset + v.shape[0]].set(v)


def kernel(x_prompt, x_sample, cache_k, cache_v, state_gdn, state_conv, page_table, meta, norm_mix, w_in,
           conv_w, a_log, dt_bias, gdn_norm, q_norm, k_norm, sb_norm, sb_bias, w_out, norm_ffn, w_gate,
           w_up, w_down):
    depth = w_in.shape[0]
    bsz, seq, d = x_prompt.shape
    dbsz, dec_seq, _ = x_sample.shape
    assert depth == 1 and bsz == 1 and dec_seq == 1
    H = d // (2 * HEAD_DIM)
    hd = H * HEAD_DIM
    n_pages, page_size = page_table.shape[1], cache_k.shape[2]
    past = n_pages * page_size
    t_valid = N_META + seq
    t_used = t_valid + dbsz
    t_pad = -(-t_used // ROW_ALIGN) * ROW_ALIGN
    l = 0

    x_all = jnp.concatenate(
        [meta.astype(f32), x_prompt[0], x_sample[:, 0], jnp.zeros((t_pad - t_used, d), f32)], axis=0)

    w = w_in[l]
    o_z = 3 * hd
    o_b = o_z + hd
    o_q = o_b + 2 * H
    w16 = w.astype(bf16)
    w_sec = jnp.concatenate([w16[:, :o_b], w16[:, o_q:]], axis=1)
    w_ba = jnp.concatenate([w16[:, o_b:o_q], jnp.zeros((d, LANES - 2 * H), bf16)], axis=1)

    p, ba, bat, kvb, kf, vf = _inproj(x_all, norm_mix[l][None], w_sec, w_ba, w_ba.T, q_norm[l][None],
                                      k_norm[l][None], H, t_valid)

    cw = conv_w[l]
    alog_r = _lane_pad(a_log[l], H)
    dt_r = _lane_pad(dt_bias[l], H)
    oa, s_fin = _gdn_prompt(p, ba, bat, cw, alog_r, dt_r,
                            jnp.broadcast_to(alog_r.T, (LANES, LANES)), jnp.broadcast_to(dt_r.T, (LANES, LANES)),
                            gdn_norm[l][None], H, t_valid)

    ps_rows = p[t_valid:t_used]
    ba_s = ba[t_valid:t_used]
    x_s = ps_rows[:, :4 * hd].reshape(dbsz, 4 * H, HEAD_DIM)
    blog = jnp.broadcast_to(ba_s[:, :H, None], (dbsz, H, LANES))
    adec = jnp.broadcast_to(ba_s[:, H:2 * H, None], (dbsz, H, LANES))
    oa_s, s_new = _gdn_step(
        x_s, state_conv[l].reshape(dbsz, CONV_W - 1, 3 * H, HEAD_DIM), cw.reshape(CONV_W, 3 * H, HEAD_DIM),
        blog, adec, jnp.broadcast_to(a_log[l][:, None], (H, LANES)),
        jnp.broadcast_to(dt_bias[l][:, None], (H, LANES)), gdn_norm[l][None], state_gdn[l], H)

    q_s = ps_rows[:, 4 * hd:5 * hd]
    k_s = ps_rows[:, 5 * hd:6 * hd]
    v_s = ps_rows[:, 6 * hd:7 * hd]
    R = page_size * H
    ridx = jnp.arange(R)
    same_head = (ridx[:, None] % H) == (ridx[None, :] % H)
    msuf = jnp.logical_and(same_head, (ridx[:, None] // H) >= (ridx[None, :] // H)).astype(bf16)
    mtot = same_head.astype(bf16)
    bias_lanes = jnp.tile(sb_bias[l], page_size)[None]
    ob, o_past = _sb_attn_prompt_and_decode(
        p, kvb, sb_bias[l], sb_norm[l][None], page_table, q_s.reshape(dbsz, H, HEAD_DIM), bias_lanes, msuf, mtot,
        cache_k[l].reshape(-1, R, HEAD_DIM), cache_v[l].reshape(-1, R, HEAD_DIM), H)
    bias_col = jnp.broadcast_to(jnp.tile(sb_bias[l], dbsz)[:, None], (dbsz * H, HEAD_DIM))
    ob_s = _dec_finish(o_past.reshape(dbsz * H, HEAD_DIM), q_s.reshape(dbsz * H, HEAD_DIM),
                       k_s.reshape(dbsz * H, HEAD_DIM), v_s.reshape(dbsz * H, HEAD_DIM),
                       bias_col, sb_norm[l][None], past, dec_seq)

    oa, ob = _place_rows(oa, ob, oa_s.reshape(dbsz, hd), ob_s.reshape(dbsz, hd), t_valid)

    wo = w_out[l].astype(bf16)
    x1, hn = _outproj(x_all, oa, ob, wo[:hd], wo[hd:], norm_ffn[l][None])
    y = _ffn(x1, hn, w_gate[l].astype(bf16), w_up[l].astype(bf16), w_down[l].astype(bf16))

    new_conv_s = jnp.concatenate([state_conv[l][:, 1:], ps_rows[:, None, :3 * hd]], axis=1)
    return (
        y[N_META:t_valid][None],
        y[t_valid:t_used][:, None],
        kf.reshape(1, 1, t_valid, H, HEAD_DIM),
        vf.reshape(1, 1, t_valid, H, HEAD_DIM),
        s_fin[None, None],
        p[t_valid - (CONV_W - 1):t_valid, :3 * hd][None, None],
        k_s.reshape(1, dbsz, 1, H, HEAD_DIM),
        v_s.reshape(1, dbsz, 1, H, HEAD_DIM),
        s_new[None],
        new_conv_s[None],
    )
```
